```python
import math
import jax, jax.numpy as jnp
from jax import lax
import numpy as np

D_MODEL = 1024
BATCH = 8
SEQ = 2048
DEPTH = 1
DEC_BATCH = 128
DEC_SEQ = 1
PAST_LEN = 8192
PAGE_SIZE = 128

N_META = 16
RET_HEADS = 4
RET_DK = 128
RET_DV = 128
RET_CHUNK = 128
DSA_HEADS = 8
DSA_DH = 64
IDX_HEADS = 8
IDX_DIM = 64
TOPK_MAX = 256
Q_BLOCK = 128
REL_BUCKETS = 32
REL_MAX_EXACT = 16
REL_MAX_DIST = 128
PEER_HEADS = 8
PEER_NKEYS = 128
PEER_N = PEER_NKEYS * PEER_NKEYS
PEER_DKEY = 256
PEER_TOPK = 16
PEER_BLOCK = 128
DN_ALPHA = (2.0 * DEPTH) ** 0.25
DN_BETA = (8.0 * DEPTH) ** -0.25
LN_EPS = 1e-5

RET_W = RET_HEADS * RET_DV
DSA_W = DSA_HEADS * DSA_DH
D_MIX = RET_W + DSA_W
PROJ_SIZES = (RET_HEADS * RET_DK, RET_HEADS * RET_DK, RET_W, RET_W, DSA_W, DSA_W, DSA_W, IDX_HEADS * IDX_DIM, IDX_DIM, IDX_HEADS)
D_PROJ = sum(PROJ_SIZES)

kernel_name = 'hymba_retnet_dsa_peer_step'


def layer_norm(x, g=None, b=None):
    xf = x.astype(jnp.float32)
    mu = jnp.mean(xf, -1, keepdims=True)
    var = jnp.mean(jnp.square(xf - mu), -1, keepdims=True)
    y = (xf - mu) * lax.rsqrt(var + LN_EPS)
    if g is not None:
        y = y * g.astype(jnp.float32) + b.astype(jnp.float32)
    return y.astype(x.dtype)


def split_proj(p):
    parts = []
    off = 0
    for n in PROJ_SIZES:
        parts.append(p[..., off:off + n])
        off += n
    return parts


def rotary(x, pos):
    half = x.shape[-1] // 2
    inv = 1.0 / (10000.0 ** (jnp.arange(half, dtype=jnp.float32) / half))
    ang = pos.astype(jnp.float32)[:, None] * inv[None, :]
    cos = jnp.cos(ang)[:, None, :]
    sin = jnp.sin(ang)[:, None, :]
    x1 = x[..., :half].astype(jnp.float32)
    x2 = x[..., half:].astype(jnp.float32)
    return jnp.concatenate([x1 * cos - x2 * sin, x1 * sin + x2 * cos], -1)


def ret_log_gamma():
    return jnp.log(1.0 - 2.0 ** (-5.0 - jnp.arange(RET_HEADS, dtype=jnp.float32)))


def retention_chunk(q, k, v, S, lg):
    C = q.shape[1]
    i = jnp.arange(C, dtype=jnp.float32)
    diff = i[:, None] - i[None, :]
    dmask = jnp.exp(jnp.where(diff[None] >= 0, diff[None] * lg[:, None, None], -jnp.inf))
    sc = jnp.einsum('bihd,bjhd->bhij', q, k) * dmask[None]
    inner = jnp.einsum('bhij,bjhv->bihv', sc, v)
    cross = jnp.einsum('bihd,bhdv->bihv', q, S) * jnp.exp((i[:, None] + 1.0) * lg[None, :])[None, :, :, None]
    kd = k * jnp.exp((C - 1.0 - i)[:, None] * lg[None, :])[None, :, :, None]
    S_new = jnp.exp(C * lg)[None, :, None, None] * S + jnp.einsum('bjhd,bjhv->bhdv', kd, v)
    return inner + cross, S_new


def retention_prompt(q, k, v, pos):
    B, T = q.shape[:2]
    lg = ret_log_gamma()
    q = rotary(q, pos) * RET_DK ** -0.5
    k = rotary(k, pos)
    v = v.astype(jnp.float32)
    S0 = jnp.zeros((B, RET_HEADS, RET_DK, RET_DV), jnp.float32)
    o_meta, S = retention_chunk(q[:, :N_META], k[:, :N_META], v[:, :N_META], S0, lg)
    nc = (T - N_META) // RET_CHUNK

    def to_chunks(a):
        return a[:, N_META:].reshape((B, nc, RET_CHUNK) + a.shape[2:]).swapaxes(0, 1)

    def step(S, blk):
        o, S = retention_chunk(blk[0], blk[1], blk[2], S, lg)
        return S, o

    S, o_rest = lax.scan(step, S, (to_chunks(q), to_chunks(k), to_chunks(v)))
    o_rest = o_rest.swapaxes(0, 1).reshape((B, T - N_META, RET_HEADS, RET_DV))
    return jnp.concatenate([o_meta, o_rest], 1), S


def retention_sample(q, k, v, pos, S0):
    lg = ret_log_gamma()
    q = rotary(q, pos) * RET_DK ** -0.5
    k = rotary(k, pos)
    return retention_chunk(q, k, v.astype(jnp.float32), S0.astype(jnp.float32), lg)


def ret_output(o, g):
    on = layer_norm(o).reshape(o.shape[:2] + (RET_W,))
    return (on * jax.nn.silu(g.astype(jnp.float32))).astype(g.dtype)


def t5_bucket(dist):
    d = jnp.maximum(dist, 0)
    lb = REL_MAX_EXACT + (jnp.log(jnp.maximum(d, 1).astype(jnp.float32) / REL_MAX_EXACT)
                          / math.log(REL_MAX_DIST / REL_MAX_EXACT) * (REL_BUCKETS - REL_MAX_EXACT)).astype(jnp.int32)
    return jnp.where(d < REL_MAX_EXACT, d, jnp.minimum(lb, REL_BUCKETS - 1))


def index_scores(qi, wi, ki):
    s = jax.nn.relu(jnp.einsum('bqhd,bld->bqhl', qi, ki).astype(jnp.float32))
    return jnp.einsum('bqh,bqhl->bql', wi.astype(jnp.float32), s)


def select_keys(scores, q_pos, n_sel):
    L = scores.shape[-1]
    visible = jnp.arange(L, dtype=jnp.int32)[None, None, :] <= q_pos[..., None]
    _, idx = lax.top_k(jnp.where(visible, scores, -jnp.inf), n_sel)
    return idx


def sparse_attend(q, k_sel, v_sel, q_pos, k_pos, rel_bias):
    logits = jnp.einsum('bqhd,bqkhd->bqhk', q, k_sel).astype(jnp.float32) * DSA_DH ** -0.5
    dist = q_pos[..., None] - k_pos
    bias = jnp.moveaxis(rel_bias[t5_bucket(dist)], -1, 2).astype(jnp.float32)
    logits = jnp.where((dist >= 0)[:, :, None, :], logits + bias, -jnp.inf)
    p = jax.nn.softmax(logits, -1).astype(v_sel.dtype)
    return jnp.einsum('bqhk,bqkhd->bqhd', p, v_sel)


def gather_rows(a, idx):
    return jax.vmap(lambda ab, ib: ab[ib])(a, idx)


def dsa_prompt(q, k, v, qi, ki, wi, rel_bias):
    B, T = q.shape[:2]
    n_sel = min(TOPK_MAX, T // 4)
    nb = -(-T // Q_BLOCK)
    Tp = nb * Q_BLOCK

    def blk(a):
        a = jnp.pad(a, [(0, 0), (0, Tp - T)] + [(0, 0)] * (a.ndim - 2))
        return a.reshape((B, nb, Q_BLOCK) + a.shape[2:]).swapaxes(0, 1)

    def one_block(args):
        qb, qib, wib, start = args
        q_pos = (start + jnp.arange(Q_BLOCK, dtype=jnp.int32))[None, :]
        idx = select_keys(index_scores(qib, wib, ki), q_pos, n_sel)
        k_sel = gather_rows(k, idx)
        v_sel = gather_rows(v, idx)
        return sparse_attend(qb, k_sel, v_sel, q_pos, idx, rel_bias)

    starts = jnp.arange(nb, dtype=jnp.int32) * Q_BLOCK
    o = lax.map(one_block, (blk(q), blk(qi), blk(wi), starts))
    return o.swapaxes(0, 1).reshape((B, Tp, DSA_HEADS, DSA_DH))[:, :T]


def dsa_sample(q, k, v, qi, ki, wi, cache_k, cache_v, cache_kidx, page_table, layer, rel_bias):
    Bd, S = q.shape[:2]
    P = cache_kidx.shape[2]
    past = page_table.shape[1] * P
    n_sel = min(TOPK_MAX, (past + S) // 4)
    ki_past = cache_kidx[layer, page_table].reshape((Bd, past, IDX_DIM))
    ki_all = jnp.concatenate([ki_past, ki.astype(ki_past.dtype)], 1)
    q_pos = past + jnp.arange(S, dtype=jnp.int32)[None, :]
    idx = select_keys(index_scores(qi, wi, ki_all), q_pos, n_sel)
    in_past = (idx < past)[..., None, None]
    pidx = jnp.minimum(idx, past - 1)
    phys = gather_rows(page_table, pidx // P)
    slot = pidx % P
    nidx = jnp.clip(idx - past, 0, S - 1)
    k_sel = jnp.where(in_past, cache_k[layer, phys, slot], gather_rows(k, nidx).astype(cache_k.dtype))
    v_sel = jnp.where(in_past, cache_v[layer, phys, slot], gather_rows(v, nidx).astype(cache_v.dtype))
    return sparse_attend(q, k_sel, v_sel, q_pos, idx, rel_bias)


def peer_ffn(x, wq, subkeys, u_tab, v_tab):
    n = x.shape[0]
    nb = -(-n // PEER_BLOCK)
    xp = jnp.pad(x, ((0, nb * PEER_BLOCK - n), (0, 0))).reshape((nb, PEER_BLOCK, D_MODEL))
    half = PEER_DKEY // 2

    def one_block(xb):
        qh = layer_norm((xb @ wq).reshape((PEER_BLOCK, PEER_HEADS, PEER_DKEY)))
        s1 = jnp.einsum('rhd,hnd->rhn', qh[..., :half], subkeys[:, 0]).astype(jnp.float32)
        s2 = jnp.einsum('rhd,hnd->rhn', qh[..., half:], subkeys[:, 1]).astype(jnp.float32)
        v1, i1 = lax.top_k(s1, PEER_TOPK)
        v2, i2 = lax.top_k(s2, PEER_TOPK)
        cand = (v1[..., :, None] + v2[..., None, :]).reshape((PEER_BLOCK, PEER_HEADS, PEER_TOPK * PEER_TOPK))
        cidx = (i1[..., :, None] * PEER_NKEYS + i2[..., None, :]).reshape((PEER_BLOCK, PEER_HEADS, PEER_TOPK * PEER_TOPK))
        top, pos = lax.top_k(cand, PEER_TOPK)
        expert = jnp.take_along_axis(cidx, pos, -1)
        g = jax.nn.softmax(top, -1)
        a = jax.nn.gelu(jnp.einsum('rd,rhkd->rhk', xb, u_tab[expert]).astype(jnp.float32), approximate=False)
        return jnp.einsum('rhk,rhkd->rd', (g * a).astype(xb.dtype), v_tab[expert])

    return lax.map(one_block, xp).reshape((nb * PEER_BLOCK, D_MODEL))[:n]


def mixer_inputs(h, w):
    B, T = h.shape[:2]
    rq, rk, rv, rg, dq, dk, dv, iq, ik, iw = split_proj(h @ w)
    return (rq.reshape((B, T, RET_HEADS, RET_DK)), rk.reshape((B, T, RET_HEADS, RET_DK)),
            rv.reshape((B, T, RET_HEADS, RET_DV)), rg,
            dq.reshape((B, T, DSA_HEADS, DSA_DH)), dk.reshape((B, T, DSA_HEADS, DSA_DH)),
            dv.reshape((B, T, DSA_HEADS, DSA_DH)), iq.reshape((B, T, IDX_HEADS, IDX_DIM)), ik, iw)


def layer_tail(h, ret_y, dsa_o, w_out, g1, b1, g2, b2, wq, subkeys, u_tab, v_tab):
    B, T = h.shape[:2]
    mix = jnp.concatenate([ret_y, dsa_o.reshape((B, T, DSA_W)).astype(ret_y.dtype)], -1) @ w_out
    h1 = layer_norm(DN_ALPHA * h + mix, g1, b1)
    f = peer_ffn(h1.reshape((B * T, D_MODEL)), wq, subkeys, u_tab, v_tab).reshape((B, T, D_MODEL))
    return layer_norm(DN_ALPHA * h1 + f, g2, b2)


def setup_inputs(seed: int = 0) -> dict:
    key = jax.random.key(seed)
    ks = jax.random.split(key, 20)
    n_pages = PAST_LEN // PAGE_SIZE
    n_used = DEC_BATCH * n_pages
    n_pool = n_used + n_used // 4
    nrm = jax.random.normal
    col_scale = jnp.concatenate([
        jnp.full((n,), D_MODEL ** -0.5 * (DN_BETA if j in (2, 6) else 1.0), jnp.float32)
        for j, n in enumerate(PROJ_SIZES)])
    return {
        'x_prompt': nrm(ks[0], (BATCH, SEQ, D_MODEL), jnp.float32),
        'x_sample': nrm(ks[1], (DEC_BATCH, DEC_SEQ, D_MODEL), jnp.float32),
        'cache_k': nrm(ks[2], (DEPTH, n_pool, PAGE_SIZE, DSA_HEADS, DSA_DH), jnp.float32),
        'cache_v': nrm(ks[3], (DEPTH, n_pool, PAGE_SIZE, DSA_HEADS, DSA_DH), jnp.float32) * DN_BETA,
        'cache_kidx': nrm(ks[4], (DEPTH, n_pool, PAGE_SIZE, IDX_DIM), jnp.float32),
        'state_ret': nrm(ks[5], (DEPTH, DEC_BATCH, RET_HEADS, RET_DK, RET_DV), jnp.float32),
        'page_table': jax.random.permutation(ks[6], n_pool)[:n_used].reshape((DEC_BATCH, n_pages)).astype(jnp.int32),
        'meta_tokens': nrm(ks[7], (N_META, D_MODEL), jnp.float32),
        'rel_bias': nrm(ks[8], (REL_BUCKETS, DSA_HEADS), jnp.float32) * 0.5,
        'w_in': nrm(ks[9], (DEPTH, D_MODEL, D_PROJ), jnp.float32) * col_scale,
        'w_out': nrm(ks[10], (DEPTH, D_MIX, D_MODEL), jnp.float32) * (D_MIX ** -0.5 * DN_BETA),
        'ln1_g': 1.0 + 0.01 * nrm(ks[11], (DEPTH, D_MODEL), jnp.float32),
        'ln1_b': 0.01 * nrm(ks[12], (DEPTH, D_MODEL), jnp.float32),
        'ln2_g': 1.0 + 0.01 * nrm(ks[13], (DEPTH, D_MODEL), jnp.float32),
        'ln2_b': 0.01 * nrm(ks[14], (DEPTH, D_MODEL), jnp.float32),
        'peer_wq': nrm(ks[15], (DEPTH, D_MODEL, PEER_HEADS * PEER_DKEY), jnp.float32) * D_MODEL ** -0.5,
        'peer_subkeys': nrm(ks[16], (DEPTH, PEER_HEADS, 2, PEER_NKEYS, PEER_DKEY // 2), jnp.float32) * (PEER_DKEY // 2) ** -0.5,
        'peer_u': nrm(ks[17], (DEPTH, PEER_N, D_MODEL), jnp.float32) * D_MODEL ** -0.5,
        'peer_v': nrm(ks[18], (DEPTH, PEER_N, D_MODEL), jnp.float32) * DN_BETA,
    }


def reference(x_prompt, x_sample, cache_k, cache_v, cache_kidx, state_ret, page_table, meta_tokens, rel_bias,
              w_in, w_out, ln1_g, ln1_b, ln2_g, ln2_b, peer_wq, peer_subkeys, peer_u, peer_v):
    B = x_prompt.shape[0]
    hp = jnp.concatenate([jnp.broadcast_to(meta_tokens[None].astype(x_prompt.dtype), (B, N_META, D_MODEL)), x_prompt], 1)
    hs = x_sample
    T = hp.shape[1]
    S = hs.shape[1]
    past = page_table.shape[1] * PAGE_SIZE
    pos_p = jnp.arange(T, dtype=jnp.int32)
    pos_s = past + jnp.arange(S, dtype=jnp.int32)
    k_p, v_p, ki_p, r_p = [], [], [], []
    k_s, v_s, ki_s, r_s = [], [], [], []
    for l in range(DEPTH):
        tail = (w_out[l], ln1_g[l], ln1_b[l], ln2_g[l], ln2_b[l], peer_wq[l], peer_subkeys[l], peer_u[l], peer_v[l])
        rq, rk, rv, rg, dq, dk, dv, iq, ik, iw = mixer_inputs(hp, w_in[l])
        ret_o, ret_S = retention_prompt(rq, rk, rv, pos_p)
        dsa_o = dsa_prompt(dq, dk, dv, iq, ik, iw, rel_bias)
        k_p.append(dk)
        v_p.append(dv)
        ki_p.append(ik)
        r_p.append(ret_S.astype(state_ret.dtype))
        hp = layer_tail(hp, ret_output(ret_o, rg), dsa_o, *tail)
        rq, rk, rv, rg, dq, dk, dv, iq, ik, iw = mixer_inputs(hs, w_in[l])
        ret_o, ret_S = retention_sample(rq, rk, rv, pos_s, state_ret[l])
        dsa_o = dsa_sample(dq, dk, dv, iq, ik, iw, cache_k, cache_v, cache_kidx, page_table, l, rel_bias)
        k_s.append(dk)
        v_s.append(dv)
        ki_s.append(ik)
        r_s.append(ret_S.astype(state_ret.dtype))
        hs = layer_tail(hs, ret_output(ret_o, rg), dsa_o, *tail)
    return (hp[:, N_META:], hs, jnp.stack(k_p), jnp.stack(v_p), jnp.stack(ki_p), jnp.stack(r_p),
            jnp.stack(k_s), jnp.stack(v_s), jnp.stack(ki_s), jnp.stack(r_s))
```

```python
import functools
import math

import numpy as np
import jax
import jax.numpy as jnp
from jax import lax
from jax.experimental import pallas as pl
from jax.experimental.pallas import tpu as pltpu

F32 = jnp.float32
BF16 = jnp.bfloat16
I32 = jnp.int32

D_MODEL = 1024
N_META = 16
RET_HEADS = 4
RET_DK = 128
DSA_HEADS = 8
DSA_DH = 64
IDX_DIM = 64
TOPK_MAX = 256
REL_BUCKETS = 32
REL_MAX_EXACT = 16
REL_MAX_DIST = 128
PEER_HEADS = 8
PEER_NKEYS = 128
PEER_DKEY = 256
PEER_TOPK = 16
HK = PEER_HEADS * PEER_TOPK
DN_ALPHA = 2.0 ** 0.25
LN_EPS = 1e-5
PROJ_MAIN = 4096
PROJ_TAIL = 128

LANES = 128
INT_MIN = -2 ** 31
VMEM_LIMIT = 56 * 1024 * 1024

NT = (((1,), (1,)), ((), ()))


def _params(sem, vmem=VMEM_LIMIT):
    return pltpu.CompilerParams(dimension_semantics=sem, vmem_limit_bytes=vmem)


def _ln(x):
    mu = jnp.mean(x, axis=-1, keepdims=True)
    xc = x - mu
    var = jnp.mean(xc * xc, axis=-1, keepdims=True)
    return xc * lax.rsqrt(var + LN_EPS)


def _inproj_kernel(x_ref, wa_ref, wt_ref, *out_refs):
    x = x_ref[...].astype(BF16)
    for j in range(8):
        out_refs[j][...] = jnp.dot(x, wa_ref[:, j * 512:(j + 1) * 512], preferred_element_type=F32)
    out_refs[8][...] = jnp.dot(x, wt_ref[...], preferred_element_type=F32)


def _inproj(x, wa, wt, tm):
    n = x.shape[0]
    row = lambda i: (i, 0)
    fixed = lambda i: (0, 0)
    return pl.pallas_call(
        _inproj_kernel,
        grid=(n // tm,),
        in_specs=[pl.BlockSpec((tm, D_MODEL), row), pl.BlockSpec((D_MODEL, PROJ_MAIN), fixed),
                  pl.BlockSpec((D_MODEL, PROJ_TAIL), fixed)],
        out_specs=[pl.BlockSpec((tm, 512), row)] * 8 + [pl.BlockSpec((tm, PROJ_TAIL), row)],
        out_shape=[jax.ShapeDtypeStruct((n, 512), F32)] * 8 + [jax.ShapeDtypeStruct((n, PROJ_TAIL), F32)],
        compiler_params=_params(("parallel",)),
        name="inproj",
    )(x, wa, wt)


def _ret_gammas():
    return [float(np.exp(np.log(np.float32(1.0 - 2.0 ** (-5.0 - h))))) for h in range(RET_HEADS)]


def _ret_tables(last_rows):
    c = LANES
    lg = np.log(1.0 - 2.0 ** (-5.0 - np.arange(RET_HEADS, dtype=np.float64)))
    i = np.arange(c, dtype=np.float64)
    diff = i[:, None] - i[None, :]
    dmask = np.where(diff[None] >= 0, np.exp(np.maximum(diff[None], 0.0) * lg[:, None, None]), 0.0)
    cdec = np.exp((i[None, :] + 1.0) * lg[:, None])[:, :, None] * np.ones((1, 1, c))
    kfull = np.exp((c - 1.0 - i)[None, :] * lg[:, None])
    klast = np.where(i[None, :] < last_rows, np.exp(np.maximum(last_rows - 1.0 - i, 0.0)[None, :] * lg[:, None]), 0.0)
    kdec = np.stack([kfull, klast])[:, :, :, None] * np.ones((1, 1, 1, c))
    g_full = [float(np.exp(c * l)) for l in lg]
    g_last = [float(np.exp(last_rows * l)) for l in lg]
    return (jnp.asarray(dmask, F32), jnp.asarray(cdec, F32), jnp.asarray(kdec, F32), g_full, g_last)


def _rot_tables(pos):
    half = RET_DK // 2
    inv = 1.0 / (10000.0 ** (jnp.arange(half, dtype=F32) / half))
    ang = pos.astype(F32)[:, None] * inv[None, :]
    cos, sin = jnp.cos(ang), jnp.sin(ang)
    return jnp.concatenate([cos, cos], -1), jnp.concatenate([-sin, sin], -1)


def _ret_prompt_kernel(q_ref, k_ref, v_ref, g_ref, cos_ref, sin_ref, dmask_ref, cdec_ref, kdec_ref,
                       y_ref, s_out_ref, s_scr, *, t_len, g_full, g_last):
    c = pl.program_id(1)
    is_last = c == pl.num_programs(1) - 1

    @pl.when(c == 0)
    def _():
        s_scr[...] = jnp.zeros_like(s_scr)

    row = lax.broadcasted_iota(I32, (LANES, 1), 0) + c * LANES
    valid = row < t_len
    cosf = cos_ref[...]
    sins = sin_ref[...]
    scale = RET_DK ** -0.5
    for h in range(RET_HEADS):
        sl = slice(h * RET_DK, (h + 1) * RET_DK)
        q = q_ref[:, sl]
        k = k_ref[:, sl]
        qr = jnp.where(valid, (q * cosf + pltpu.roll(q, RET_DK // 2, 1) * sins) * scale, 0.0)
        kr = jnp.where(valid, k * cosf + pltpu.roll(k, RET_DK // 2, 1) * sins, 0.0)
        v = jnp.where(valid, v_ref[:, sl], 0.0)
        qb = qr.astype(BF16)
        kb = kr.astype(BF16)
        vb = v.astype(BF16)
        sc = lax.dot_general(qb, kb, NT, preferred_element_type=F32) * dmask_ref[h]
        inner = jnp.dot(sc.astype(BF16), vb, preferred_element_type=F32)
        s_old = s_scr[h]
        cross = jnp.dot(qb, s_old.astype(BF16), preferred_element_type=F32) * cdec_ref[h]
        o = inner + cross
        kd_t = (kr * kdec_ref[h]).T.astype(BF16)
        gdec = jnp.where(is_last, g_last[h], g_full[h])
        s_scr[h] = gdec * s_old + jnp.dot(kd_t, vb, preferred_element_type=F32)
        g = g_ref[:, sl]
        y_ref[:, sl] = (_ln(o) * (g * jax.nn.sigmoid(g))).astype(BF16)

    @pl.when(is_last)
    def _():
        s_out_ref[...] = s_scr[...]


def _ret_prompt(rq, rk, rv, rg, cosf, sins):
    b, t, _ = rq.shape
    nc = pl.cdiv(t, LANES)
    last_rows = t - (nc - 1) * LANES
    dmask, cdec, kdec, g_full, g_last = _ret_tables(last_rows)
    blk = pl.BlockSpec((None, LANES, 512), lambda bi, ci: (bi, ci, 0))
    tab = pl.BlockSpec((LANES, RET_DK), lambda bi, ci: (ci, 0))
    full3 = pl.BlockSpec((RET_HEADS, LANES, LANES), lambda bi, ci: (0, 0, 0))
    return pl.pallas_call(
        functools.partial(_ret_prompt_kernel, t_len=t, g_full=g_full, g_last=g_last),
        grid=(b, nc),
        in_specs=[blk, blk, blk, blk, tab, tab, full3, full3,
                  pl.BlockSpec((None, RET_HEADS, LANES, LANES), lambda bi, ci: (ci // (nc - 1), 0, 0, 0))],
        out_specs=[blk, pl.BlockSpec((None, RET_HEADS, RET_DK, RET_DK), lambda bi, ci: (bi, 0, 0, 0))],
        out_shape=[jax.ShapeDtypeStruct((b, t, 512), BF16),
                   jax.ShapeDtypeStruct((b, RET_HEADS, RET_DK, RET_DK), F32)],
        scratch_shapes=[pltpu.VMEM((RET_HEADS, RET_DK, RET_DK), F32)],
        compiler_params=_params(("parallel", "arbitrary")),
        name="ret_prompt",
    )(rq, rk, rv, rg, cosf, sins, dmask, cdec, kdec)


def _ret_sample_kernel(q_ref, k_ref, qt_ref, kt_ref, v_ref, g_ref, cos_ref, sin_ref, cost_ref, sint_ref,
                       s0_ref, y_ref, s1_ref, o_scr, *, gammas):
    h = pl.program_id(0)
    gamma = jnp.float32(gammas[0])
    for i in range(1, RET_HEADS):
        gamma = jnp.where(h == i, jnp.float32(gammas[i]), gamma)
    scale = RET_DK ** -0.5
    half = RET_DK // 2

    def rot_t(x):
        return x * cost_ref[...] + jnp.concatenate([x[half:], x[:half]], axis=0) * sint_ref[...]

    def rot(x):
        return x * cos_ref[...] + pltpu.roll(x, half, 1) * sin_ref[...]

    def r16(x):
        return x.astype(BF16).astype(F32)

    qt = r16(rot_t(qt_ref[...]) * scale)
    kt = r16(rot_t(kt_ref[...]))
    q = r16(rot(q_ref[...]) * scale)
    k = r16(rot(k_ref[...]))
    vb = r16(v_ref[...])
    qk = r16(jnp.sum(q * k, axis=1, keepdims=True))
    inner = qk * vb
    for b in range(q.shape[0]):
        s_old = s0_ref[b]
        qc = qt[:, b:b + 1]
        kc = kt[:, b:b + 1]
        o_scr[b:b + 1, :] = jnp.sum(qc * r16(s_old), axis=0, keepdims=True) * gamma
        s1_ref[b] = gamma * s_old + kc * vb[b:b + 1, :]
    o = inner + o_scr[...]
    g = g_ref[...]
    y_ref[...] = (_ln(o) * (g * jax.nn.sigmoid(g))).astype(BF16)


def _ret_sample(rq, rk, rv, rg, state, pos):
    nb = rq.shape[0]
    cosf, sins = _rot_tables(jnp.full((1,), pos, I32))
    cos_b = jnp.broadcast_to(cosf, (nb, RET_DK))
    sin_b = jnp.broadcast_to(sins, (nb, RET_DK))
    cos_t = jnp.broadcast_to(cosf.reshape(RET_DK, 1), (RET_DK, nb))
    sin_t = jnp.broadcast_to(sins.reshape(RET_DK, 1), (RET_DK, nb))
    qt = rq.reshape(nb, RET_HEADS, RET_DK).transpose(1, 2, 0)
    kt = rk.reshape(nb, RET_HEADS, RET_DK).transpose(1, 2, 0)
    col = pl.BlockSpec((nb, RET_DK), lambda h: (0, h))
    tr = pl.BlockSpec((None, RET_DK, nb), lambda h: (h, 0, 0))
    full = pl.BlockSpec((nb, RET_DK), lambda h: (0, 0))
    full_t = pl.BlockSpec((RET_DK, nb), lambda h: (0, 0))
    st = pl.BlockSpec((nb, None, RET_DK, RET_DK), lambda h: (0, h, 0, 0))
    return pl.pallas_call(
        functools.partial(_ret_sample_kernel, gammas=_ret_gammas()),
        grid=(RET_HEADS,),
        in_specs=[col, col, tr, tr, col, col, full, full, full_t, full_t, st],
        out_specs=[col, st],
        out_shape=[jax.ShapeDtypeStruct((nb, 512), BF16), jax.ShapeDtypeStruct(state.shape, F32)],
        scratch_shapes=[pltpu.VMEM((nb, RET_DK), F32)],
        compiler_params=_params(("parallel",)),
        name="ret_sample",
    )(rq, rk, qt, kt, rv, rg, cos_b, sin_b, cos_t, sin_t, state)


def _bucket_np(d):
    d = np.maximum(d, 0)
    lb = REL_MAX_EXACT + (np.log(np.maximum(d, 1).astype(np.float32) / np.float32(REL_MAX_EXACT))
                          / np.float32(math.log(REL_MAX_DIST / REL_MAX_EXACT))
                          * np.float32(REL_BUCKETS - REL_MAX_EXACT)).astype(np.int32)
    return np.where(d < REL_MAX_EXACT, d, np.minimum(lb, REL_BUCKETS - 1)).astype(np.int32)


def _bucket_bounds():
    d = np.arange(0, 4 * REL_MAX_DIST)
    bk = _bucket_np(d)
    assert np.all(np.diff(bk) >= 0) and bk[-1] == REL_BUCKETS - 1
    return [int(d[np.argmax(bk >= b)]) for b in range(REL_BUCKETS)]


def _bias_tiles_kernel(rb_ref, bidx_ref, out_ref):
    for m in range(2):
        bi = bidx_ref[m]
        for h in range(DSA_HEADS):
            acc = jnp.zeros((LANES, LANES), F32)
            for b in range(REL_BUCKETS):
                acc = jnp.where(bi == b, rb_ref[b, h], acc)
            out_ref[h, m] = acc


def _bias_tiles(rel_bias):
    r = np.arange(LANES)
    d = r[:, None] - r[None, :]
    bidx = jnp.asarray(np.stack([_bucket_np(d), _bucket_np(d + LANES)]), I32)
    assert int(_bucket_np(np.array([LANES + 1]))[0]) == REL_BUCKETS - 1
    return pl.pallas_call(
        _bias_tiles_kernel,
        in_specs=[pl.BlockSpec(memory_space=pltpu.SMEM), pl.BlockSpec(memory_space=pltpu.VMEM)],
        out_specs=pl.BlockSpec(memory_space=pltpu.VMEM),
        out_shape=jax.ShapeDtypeStruct((DSA_HEADS, 2, LANES, LANES), F32),
        name="bias_tiles",
    )(rel_bias, bidx)


def _order_key(x):
    bits = pltpu.bitcast(jnp.where(x == 0.0, 0.0, x), I32)
    return jnp.where(bits < 0, bits ^ jnp.int32(0x7FFFFFFF), bits)


def _count(mask):
    return jnp.sum(jnp.where(mask, 1.0, 0.0), axis=1, keepdims=True)


def _dsa_prompt_kernel(bfar_ref, dq_ref, iq_ref, tq_ref, dk_ref, dv_ref, tk_ref, bt_ref, o_ref,
                       kpad, vpad, kipad, key_scr, am_scr, lg_scr, *, t_len, n_sel, tk_pad, col_bits):
    i = pl.program_id(1)

    @pl.when(i == 0)
    def _():
        kpad[0:t_len] = dk_ref[...].astype(BF16)
        kpad[t_len:tk_pad] = jnp.zeros((tk_pad - t_len, 512), BF16)
        vpad[0:t_len] = dv_ref[...].astype(BF16)
        vpad[t_len:tk_pad] = jnp.zeros((tk_pad - t_len, 512), BF16)
        kipad[0:t_len] = tk_ref[...].astype(BF16)
        kipad[t_len:tk_pad] = jnp.zeros((tk_pad - t_len, PROJ_TAIL), BF16)

    qi = iq_ref[...].astype(BF16)
    w = tq_ref[:, IDX_DIM:IDX_DIM + DSA_HEADS]
    ki = kipad[:, 0:IDX_DIM]
    sc = None
    for h in range(DSA_HEADS):
        d = lax.dot_general(qi[:, h * IDX_DIM:(h + 1) * IDX_DIM], ki, NT, preferred_element_type=F32)
        t = w[:, h:h + 1] * jnp.maximum(d, 0.0)
        sc = t if sc is None else sc + t

    col = lax.broadcasted_iota(I32, (LANES, tk_pad), 1)
    rowpos = i * LANES + lax.broadcasted_iota(I32, (LANES, 1), 0)
    vis = col <= rowpos
    key_scr[...] = jnp.where(vis, _order_key(sc), INT_MIN)

    def search(it, res):
        cand = res | lax.shift_left(jnp.int32(1), 31 - it)
        cnt = _count(key_scr[...] >= (cand ^ INT_MIN))
        return jnp.where(cnt >= n_sel, cand, res)

    thr = lax.fori_loop(0, 32, search, jnp.zeros((LANES, 1), I32)) ^ INT_MIN

    key = key_scr[...]
    gt = key > thr
    eq = key == thr
    need = n_sel - _count(gt)

    def tie_search(it, res):
        cand = res | lax.shift_left(jnp.int32(1), col_bits - 1 - it)
        before = jnp.sum(jnp.where(key_scr[...] == thr, jnp.where(col < cand, 1.0, 0.0), 0.0),
                         axis=1, keepdims=True)
        return jnp.where(before < need, cand, res)

    j0 = lax.fori_loop(0, col_bits, tie_search, jnp.zeros((LANES, 1), I32))
    ninf = jnp.float32(-jnp.inf)
    tie_ok = jnp.where(eq, jnp.where(col <= j0, 0.0, ninf), ninf)
    am_scr[...] = jnp.where(vis, jnp.where(gt, 0.0, tie_ok), ninf)

    q = dq_ref[...] * (DSA_DH ** -0.5)
    c0 = pl.multiple_of(i * LANES, LANES)
    c1 = pl.multiple_of(jnp.maximum(i - 1, 0) * LANES, LANES)
    for h in range(DSA_HEADS):
        hs = slice(h * DSA_DH, (h + 1) * DSA_DH)
        qh = q[:, hs].astype(BF16)
        lg_scr[...] = (lax.dot_general(qh, kpad[:, hs], NT, preferred_element_type=F32)
                       + bfar_ref[h] + am_scr[...])
        for m, cm in ((1, c1), (0, c0)):
            blk = lax.dot_general(qh, kpad[pl.ds(cm, LANES), hs], NT, preferred_element_type=F32)
            lg_scr[:, pl.ds(cm, LANES)] = blk + bt_ref[h, m] + am_scr[:, pl.ds(cm, LANES)]
        lg = lg_scr[...]
        mx = jnp.max(lg, axis=1, keepdims=True)
        p = jnp.exp(lg - mx)
        den = jnp.sum(p, axis=1, keepdims=True)
        oh = jnp.dot(p.astype(BF16), vpad[:, hs], preferred_element_type=F32) / den
        o_ref[:, hs] = oh.astype(BF16)


def _dsa_prompt(dq, dk, dv, iq, tail, bias_tiles, bias_far):
    b, t, _ = dq.shape
    n_sel = min(TOPK_MAX, t // 4)
    nq = pl.cdiv(t, LANES)
    tk_pad = nq * LANES
    col_bits = max(1, int(math.ceil(math.log2(tk_pad + 1))))
    qblk = lambda wd: pl.BlockSpec((None, LANES, wd), lambda bi, qi: (bi, qi, 0))
    kblk = lambda wd: pl.BlockSpec((None, t, wd), lambda bi, qi: (bi, 0, 0))
    return pl.pallas_call(
        functools.partial(_dsa_prompt_kernel, t_len=t, n_sel=n_sel, tk_pad=tk_pad, col_bits=col_bits),
        grid=(b, nq),
        in_specs=[pl.BlockSpec(memory_space=pltpu.SMEM), qblk(512), qblk(512), qblk(PROJ_TAIL),
                  kblk(512), kblk(512), kblk(PROJ_TAIL),
                  pl.BlockSpec((DSA_HEADS, 2, LANES, LANES), lambda bi, qi: (0, 0, 0, 0))],
        out_specs=qblk(512),
        out_shape=jax.ShapeDtypeStruct((b, t, 512), BF16),
        scratch_shapes=[pltpu.VMEM((tk_pad, 512), BF16), pltpu.VMEM((tk_pad, 512), BF16),
                        pltpu.VMEM((tk_pad, PROJ_TAIL), BF16), pltpu.VMEM((LANES, tk_pad), I32),
                        pltpu.VMEM((LANES, tk_pad), F32), pltpu.VMEM((LANES, tk_pad), F32)],
        compiler_params=_params(("parallel", "arbitrary")),
        name="dsa_prompt",
    )(bias_far, dq, iq, tail, dk, dv, tail, bias_tiles)


def _idx_scores_kernel(pt_ref, qi_ref, w_ref, kin_ref, kidx_hbm, out_ref, buf, sem, *, n_pages, page, past):
    b = pl.program_id(0)

    def page_copy(p, src_page):
        return pltpu.make_async_copy(kidx_hbm.at[src_page], buf.at[p], sem)

    def start(p, c):
        page_copy(p, pt_ref[b, p]).start()
        return c

    lax.fori_loop(0, n_pages, start, 0)
    buf[n_pages] = jnp.zeros((page, IDX_DIM), F32)
    buf[n_pages, 0:1, :] = kin_ref[...]

    def wait(p, c):
        page_copy(p, 0).wait()
        return c

    lax.fori_loop(0, n_pages, wait, 0)
    kall = buf[...].reshape((n_pages + 1) * page, IDX_DIM).astype(BF16)
    d = lax.dot_general(qi_ref[...].astype(BF16), kall, NT, preferred_element_type=F32)
    s = jnp.sum(w_ref[...] * jnp.maximum(d, 0.0), axis=0, keepdims=True)
    col = lax.broadcasted_iota(I32, s.shape, 1)
    out_ref[...] = jnp.where(col <= past, s, -jnp.inf)


def _idx_scores(page_table, qi, w, ki_new, cache_kidx):
    nb, n_pages = page_table.shape
    page = cache_kidx.shape[1]
    past = n_pages * page
    width = past + page
    grid_spec = pltpu.PrefetchScalarGridSpec(
        num_scalar_prefetch=1,
        grid=(nb,),
        in_specs=[pl.BlockSpec((None, DSA_HEADS, IDX_DIM), lambda b, pt: (b, 0, 0)),
                  pl.BlockSpec((None, DSA_HEADS, 1), lambda b, pt: (b, 0, 0)),
                  pl.BlockSpec((None, 1, IDX_DIM), lambda b, pt: (b, 0, 0)),
                  pl.BlockSpec(memory_space=pl.ANY)],
        out_specs=pl.BlockSpec((None, 1, width), lambda b, pt: (b, 0, 0)),
        scratch_shapes=[pltpu.VMEM((n_pages + 1, page, IDX_DIM), F32), pltpu.SemaphoreType.DMA(())],
    )
    return pl.pallas_call(
        functools.partial(_idx_scores_kernel, n_pages=n_pages, page=page, past=past),
        grid_spec=grid_spec,
        out_shape=jax.ShapeDtypeStruct((nb, 1, width), F32),
        compiler_params=_params(("arbitrary",)),
        name="idx_scores",
    )(page_table, qi, w, ki_new, cache_kidx)


def _topk_idx_kernel(s_ref, idx_ref, s_scr, *, n_sel):
    nb, width = s_ref.shape
    s_scr[...] = s_ref[...]
    col = lax.broadcasted_iota(I32, (nb, width), 1).astype(F32)
    slot = lax.broadcasted_iota(I32, (nb, n_sel), 1)

    def body(k, carry):
        prev, acc = carry
        s = jnp.where(col == prev, -jnp.inf, s_scr[...])
        s_scr[...] = s
        m = jnp.max(s, axis=1, keepdims=True)
        first = jnp.min(jnp.where(s == m, col, 1e9), axis=1, keepdims=True)
        return first, jnp.where(slot == k, first, acc)

    _, acc = lax.fori_loop(0, n_sel, body, (jnp.full((nb, 1), -1.0, F32), jnp.zeros((nb, n_sel), F32)))
    idx_ref[...] = acc.astype(I32)


def _topk_idx(scores, n_sel):
    nb, width = scores.shape
    return pl.pallas_call(
        functools.partial(_topk_idx_kernel, n_sel=n_sel),
        in_specs=[pl.BlockSpec(memory_space=pltpu.VMEM)],
        out_specs=pl.BlockSpec(memory_space=pltpu.VMEM),
        out_shape=jax.ShapeDtypeStruct((nb, n_sel), I32),
        scratch_shapes=[pltpu.VMEM((nb, width), F32)],
        compiler_params=pltpu.CompilerParams(vmem_limit_bytes=VMEM_LIMIT),
        name="topk_idx",
    )(scores)


def _dsa_sample_kernel(idx_sm, pt_sm, q_ref, idxv_ref, rb_ref, ck_hbm, cv_hbm, kn_hbm, vn_hbm, o_ref,
                       kbuf, vbuf, semk, semv, *, n_sel, past, page, bounds):
    b = pl.program_id(0)

    def row_copies(src_k, src_v, row, j):
        return (pltpu.make_async_copy(src_k.at[pl.ds(row, 1)], kbuf.at[pl.ds(j, 1)], semk),
                pltpu.make_async_copy(src_v.at[pl.ds(row, 1)], vbuf.at[pl.ds(j, 1)], semv))

    def issue(j, c):
        s = idx_sm[b, j]

        @pl.when(s < past)
        def _():
            row = pt_sm[b, s // page] * page + s % page
            for cp in row_copies(ck_hbm, cv_hbm, row, j):
                cp.start()

        @pl.when(s >= past)
        def _():
            for cp in row_copies(kn_hbm, vn_hbm, b, j):
                cp.start()

        return c

    lax.fori_loop(0, n_sel, issue, 0)

    def wait(j, c):
        for cp in row_copies(kn_hbm, vn_hbm, 0, j):
            cp.wait()
        return c

    lax.fori_loop(0, n_sel, wait, 0)

    ks = kbuf[...].astype(BF16)
    vs = vbuf[...].astype(BF16)
    dist = past - idxv_ref[...]
    bucket = jnp.zeros(dist.shape, I32)
    for bb in range(1, REL_BUCKETS):
        bucket = bucket + jnp.where(dist >= bounds[bb], 1, 0)
    q = q_ref[...] * (DSA_DH ** -0.5)
    for h in range(DSA_HEADS):
        hs = slice(h * DSA_DH, (h + 1) * DSA_DH)
        bias = jnp.zeros(dist.shape, F32)
        for bb in range(REL_BUCKETS):
            bias = jnp.where(bucket == bb, rb_ref[bb, h], bias)
        qh = jnp.broadcast_to(q[:, hs], (8, DSA_DH)).astype(BF16)
        lg = lax.dot_general(qh, ks[:, hs], NT, preferred_element_type=F32) + bias
        mx = jnp.max(lg, axis=1, keepdims=True)
        p = jnp.exp(lg - mx)
        den = jnp.sum(p, axis=1, keepdims=True)
        oh = jnp.dot(p.astype(BF16), vs[:, hs], preferred_element_type=F32) / den
        o_ref[:, hs] = oh[0:1].astype(BF16)


def _dsa_sample(idx, page_table, dq, rel_bias, cache_k, cache_v, dk_new, dv_new, page):
    nb, n_sel = idx.shape
    past = page_table.shape[1] * page
    row = lambda wd: pl.BlockSpec((None, 1, wd), lambda b, i_sm, p_sm: (b, 0, 0))
    grid_spec = pltpu.PrefetchScalarGridSpec(
        num_scalar_prefetch=2,
        grid=(nb,),
        in_specs=[row(512), row(n_sel), pl.BlockSpec(memory_space=pltpu.SMEM),
                  pl.BlockSpec(memory_space=pl.ANY), pl.BlockSpec(memory_space=pl.ANY),
                  pl.BlockSpec(memory_space=pl.ANY), pl.BlockSpec(memory_space=pl.ANY)],
        out_specs=row(512),
        scratch_shapes=[pltpu.VMEM((n_sel, 512), F32), pltpu.VMEM((n_sel, 512), F32),
                        pltpu.SemaphoreType.DMA(()), pltpu.SemaphoreType.DMA(())],
    )
    return pl.pallas_call(
        functools.partial(_dsa_sample_kernel, n_sel=n_sel, past=past, page=page, bounds=_bucket_bounds()),
        grid_spec=grid_spec,
        out_shape=jax.ShapeDtypeStruct((nb, 1, 512), BF16),
        compiler_params=_params(("arbitrary",)),
        name="dsa_sample",
    )(idx, page_table, dq.reshape(nb, 1, 512), idx.reshape(nb, 1, n_sel), rel_bias,
      cache_k, cache_v, dk_new, dv_new)


def _tail1_kernel(ry_ref, do_ref, h_ref, wo_ref, g1_ref, b1_ref, wq_ref, sk_ref, h1_ref, h1b_ref, st_ref):
    half = wo_ref.shape[0] // 2
    mix = (jnp.dot(ry_ref[...], wo_ref[0:half], preferred_element_type=F32)
           + jnp.dot(do_ref[...], wo_ref[half:], preferred_element_type=F32))
    h1 = _ln(DN_ALPHA * h_ref[...] + mix) * g1_ref[...] + b1_ref[...]
    h1_ref[...] = h1
    h1b = h1.astype(BF16)
    h1b_ref[...] = h1b
    q = jnp.dot(h1b, wq_ref[...], preferred_element_type=F32)
    for hh in range(PEER_HEADS):
        qh = _ln(q[:, hh * PEER_DKEY:(hh + 1) * PEER_DKEY]).astype(BF16)
        for s in range(2):
            qs = qh[:, s * (PEER_DKEY // 2):(s + 1) * (PEER_DKEY // 2)]
            st_ref[hh * 2 + s] = lax.dot_general(sk_ref[hh, s], qs, NT, preferred_element_type=F32)


def _tail1(ret_y, dsa_o, h, wo, g1, b1, wq, sk, tm):
    n = h.shape[0]
    row = lambda wd: pl.BlockSpec((tm, wd), lambda i: (i, 0))
    fixed = lambda shp: pl.BlockSpec(shp, lambda i: (0,) * len(shp))
    return pl.pallas_call(
        _tail1_kernel,
        grid=(n // tm,),
        in_specs=[row(512), row(512), row(D_MODEL), fixed(wo.shape), fixed((1, D_MODEL)), fixed((1, D_MODEL)),
                  fixed(wq.shape), fixed(sk.shape)],
        out_specs=[row(D_MODEL), row(D_MODEL),
                   pl.BlockSpec((2 * PEER_HEADS, PEER_NKEYS, tm), lambda i: (0, 0, i))],
        out_shape=[jax.ShapeDtypeStruct((n, D_MODEL), F32), jax.ShapeDtypeStruct((n, D_MODEL), BF16),
                   jax.ShapeDtypeStruct((2 * PEER_HEADS, PEER_NKEYS, n), F32)],
        compiler_params=_params(("parallel",)),
        name="tail1",
    )(ret_y, dsa_o, h, wo, g1, b1, wq, sk)


def _top16(s):
    kio = lax.broadcasted_iota(I32, s.shape, 0).astype(F32)
    vals, idxs = [], []
    for _ in range(PEER_TOPK):
        m = jnp.max(s, axis=0, keepdims=True)
        first = jnp.min(jnp.where(s == m, kio, float(s.shape[0])), axis=0, keepdims=True)
        s = jnp.where(kio == first, -jnp.inf, s)
        vals.append(m)
        idxs.append(first)
    return jnp.concatenate(vals, 0), jnp.concatenate(idxs, 0)


def _route_kernel(st_ref, g_ref, e_ref):
    r8 = lax.broadcasted_iota(I32, (8, LANES), 0).astype(F32)

    def head(hh, c):
        v1, i1 = _top16(st_ref[2 * hh])
        v2, i2 = _top16(st_ref[2 * hh + 1])
        vals = [v1[0:8] + v2[0:1], v1[8:16] + v2[0:1]]
        flat = [r8 * 16.0, (r8 + 8.0) * 16.0]
        exp_id = [i1[0:8] * PEER_NKEYS + i2[0:1], i1[8:16] * PEER_NKEYS + i2[0:1]]
        for j in range(1, 8):
            vals.append(v1[0:8] + v2[j:j + 1])
            flat.append(r8 * 16.0 + float(j))
            exp_id.append(i1[0:8] * PEER_NKEYS + i2[j:j + 1])
        vals.append(v1[0:1] + v2[8:16])
        flat.append(r8 + 8.0)
        exp_id.append(i1[0:1] * PEER_NKEYS + i2[8:16])
        cand = jnp.concatenate(vals, 0)
        pos = jnp.concatenate(flat, 0)
        eid = jnp.concatenate(exp_id, 0)
        tops, es = [], []
        for _ in range(PEER_TOPK):
            m = jnp.max(cand, axis=0, keepdims=True)
            pm = jnp.min(jnp.where(cand == m, pos, 1e9), axis=0, keepdims=True)
            hit = pos == pm
            es.append(jnp.max(jnp.where(hit, eid, -1.0), axis=0, keepdims=True))
            cand = jnp.where(hit, -jnp.inf, cand)
            tops.append(m)
        top = jnp.concatenate(tops, 0)
        ex = jnp.exp(top - top[0:1])
        off = pl.multiple_of(hh * PEER_TOPK, PEER_TOPK)
        g_ref[pl.ds(off, PEER_TOPK), :] = ex / jnp.sum(ex, axis=0, keepdims=True)
        e_ref[pl.ds(off, PEER_TOPK), :] = jnp.concatenate(es, 0).astype(I32)
        return c

    lax.fori_loop(0, PEER_HEADS, head, 0)


def _route(st):
    n = st.shape[2]
    blk = pl.BlockSpec((HK, LANES), lambda i: (0, i))
    return pl.pallas_call(
        _route_kernel,
        grid=(n // LANES,),
        in_specs=[pl.BlockSpec((2 * PEER_HEADS, PEER_NKEYS, LANES), lambda i: (0, 0, i))],
        out_specs=[blk, blk],
        out_shape=[jax.ShapeDtypeStruct((HK, n), F32), jax.ShapeDtypeStruct((HK, n), I32)],
        compiler_params=_params(("parallel",)),
        name="route",
    )(st)


EXPERT_BLOCK = 256


def _peer_act_kernel(x_ref, u_ref, e_ref, g_ref, w_ref, acc_scr):
    j = pl.program_id(1)

    @pl.when(j == 0)
    def _():
        acc_scr[...] = jnp.zeros_like(acc_scr)

    hmat = lax.dot_general(x_ref[...], u_ref[...], NT, preferred_element_type=F32)
    e = e_ref[...]
    i2 = e & (PEER_NKEYS - 1)
    i1 = e >> 7
    acc = acc_scr[...]
    for s in range(EXPERT_BLOCK // PEER_NKEYS):
        got = jnp.take_along_axis(hmat[:, s * PEER_NKEYS:(s + 1) * PEER_NKEYS], i2, axis=1)
        acc = jnp.where(i1 == j * (EXPERT_BLOCK // PEER_NKEYS) + s, got, acc)
    acc_scr[...] = acc

    @pl.when(j == pl.num_programs(1) - 1)
    def _():
        a = acc_scr[...]
        gelu = 0.5 * a * (1.0 + lax.erf(a * (2.0 ** -0.5)))
        w_ref[...] = g_ref[...] * gelu


def _peer_act(xb, u, e, g, tr):
    n = xb.shape[0]
    row = lambda wd: pl.BlockSpec((tr, wd), lambda i, j: (i, 0))
    return pl.pallas_call(
        _peer_act_kernel,
        grid=(n // tr, u.shape[0] // EXPERT_BLOCK),
        in_specs=[row(D_MODEL), pl.BlockSpec((EXPERT_BLOCK, D_MODEL), lambda i, j: (j, 0)), row(HK), row(HK)],
        out_specs=row(HK),
        out_shape=jax.ShapeDtypeStruct((n, HK), F32),
        scratch_shapes=[pltpu.VMEM((tr, HK), F32)],
        compiler_params=_params(("parallel", "arbitrary")),
        name="peer_act",
    )(xb, u, e, g)


def _peer_gate_kernel(e_ref, w_ref, p_ref):
    sub = lax.broadcasted_iota(I32, (PEER_NKEYS, HK), 0)

    def body(r, c):
        er = e_ref[pl.ds(r, 1), :]
        wr = w_ref[pl.ds(r, 1), :]
        o1 = jnp.where(sub == (er >> 7), 1.0, 0.0).astype(BF16)
        o2 = jnp.where(sub == (er & (PEER_NKEYS - 1)), wr, 0.0).astype(BF16)
        p_ref[r] = lax.dot_general(o1, o2, NT, preferred_element_type=F32).astype(BF16)
        return c

    lax.fori_loop(0, e_ref.shape[0], body, 0)


def _peer_gate(e, w, tb):
    n = e.shape[0]
    row = pl.BlockSpec((tb, HK), lambda i: (i, 0))
    return pl.pallas_call(
        _peer_gate_kernel,
        grid=(n // tb,),
        in_specs=[row, row],
        out_specs=pl.BlockSpec((tb, PEER_NKEYS, PEER_NKEYS), lambda i: (i, 0, 0)),
        out_shape=jax.ShapeDtypeStruct((n, PEER_NKEYS, PEER_NKEYS), BF16),
        compiler_params=_params(("parallel",)),
        name="peer_gate",
    )(e, w)


def _peer_out_kernel(p_ref, v_ref, h1_ref, g2_ref, b2_ref, y_ref, acc_scr):
    k = pl.program_id(1)

    @pl.when(k == 0)
    def _():
        acc_scr[...] = jnp.zeros_like(acc_scr)

    acc_scr[...] += jnp.dot(p_ref[...], v_ref[...], preferred_element_type=F32)

    @pl.when(k == pl.num_programs(1) - 1)
    def _():
        y_ref[...] = _ln(DN_ALPHA * h1_ref[...] + acc_scr[...]) * g2_ref[...] + b2_ref[...]


def _peer_out(p, v, h1, g2, b2, tr, kc):
    n = h1.shape[0]
    row = pl.BlockSpec((tr, D_MODEL), lambda i, k: (i, 0))
    vec = pl.BlockSpec((1, D_MODEL), lambda i, k: (0, 0))
    return pl.pallas_call(
        _peer_out_kernel,
        grid=(n // tr, p.shape[1] // kc),
        in_specs=[pl.BlockSpec((tr, kc), lambda i, k: (i, k)), pl.BlockSpec((kc, D_MODEL), lambda i, k: (k, 0)),
                  row, vec, vec],
        out_specs=row,
        out_shape=jax.ShapeDtypeStruct((n, D_MODEL), F32),
        scratch_shapes=[pltpu.VMEM((tr, D_MODEL), F32)],
        compiler_params=_params(("parallel", "arbitrary")),
        name="peer_out",
    )(p, v, h1, g2, b2)


def _layer_tail(ret_y, dsa_o, h, wts, tiles):
    wo, g1, b1, g2, b2, wq, sk, u, v = wts
    tm, tr_act, tb, tr_out, kc = tiles
    n = h.shape[0]
    h1, h1b, st = _tail1(ret_y, dsa_o, h, wo, g1, b1, wq, sk, tm)
    g_t, e_t = _route(st)
    g = g_t.T
    e = e_t.T
    w = _peer_act(h1b, u, e, g, tr_act)
    p = _peer_gate(e, w, tb).reshape(n, PEER_NKEYS * PEER_NKEYS)
    return _peer_out(p, v, h1, g2, b2, tr_out, kc)


def kernel(x_prompt, x_sample, cache_k, cache_v, cache_kidx, state_ret, page_table, meta_tokens, rel_bias,
           w_in, w_out, ln1_g, ln1_b, ln2_g, ln2_b, peer_wq, peer_subkeys, peer_u, peer_v):
    assert w_in.shape[0] == 1 and x_sample.shape[1] == 1
    nbp, seq, _ = x_prompt.shape
    t = seq + N_META
    nbs = x_sample.shape[0]
    n_pages = page_table.shape[1]
    page = cache_kidx.shape[2]
    past = n_pages * page

    hp = jnp.concatenate([jnp.broadcast_to(meta_tokens[None], (nbp, N_META, D_MODEL)), x_prompt], 1)
    hp = hp.reshape(nbp * t, D_MODEL)
    hs = x_sample.reshape(nbs, D_MODEL)

    w = w_in[0]
    wa = w[:, :PROJ_MAIN].astype(BF16)
    wt = jnp.pad(w[:, PROJ_MAIN:], ((0, 0), (0, PROJ_TAIL - (w.shape[1] - PROJ_MAIN)))).astype(BF16)
    wts = (w_out[0].astype(BF16), ln1_g, ln1_b, ln2_g, ln2_b, peer_wq[0].astype(BF16),
           peer_subkeys[0].astype(BF16), peer_u[0].astype(BF16), peer_v[0].astype(BF16))

    rq, rk, rv, rg, dq, dk, dv, iq, tail = _inproj(hp, wa, wt, 384)
    b3 = lambda a: a.reshape(nbp, t, a.shape[-1])
    cosf, sins = _rot_tables(jnp.arange(t, dtype=I32))
    ret_y, ret_s = _ret_prompt(b3(rq), b3(rk), b3(rv), b3(rg), cosf, sins)
    bias_tiles = _bias_tiles(rel_bias)
    dsa_o = _dsa_prompt(b3(dq), b3(dk), b3(dv), b3(iq), b3(tail), bias_tiles, rel_bias[REL_BUCKETS - 1])
    yp = _layer_tail(ret_y.reshape(nbp * t, 512), dsa_o.reshape(nbp * t, 512), hp, wts,
                     (384, 1376, 64, 688, 2048))
    y_prompt = yp.reshape(nbp, t, D_MODEL)[:, N_META:]

    srq, srk, srv, srg, sdq, sdk, sdv, siq, stail = _inproj(hs, wa, wt, nbs)
    sret_y, sret_s = _ret_sample(srq, srk, srv, srg, state_ret[0], past)
    scores = _idx_scores(page_table, siq.reshape(nbs, DSA_HEADS, IDX_DIM),
                         stail[:, IDX_DIM:IDX_DIM + DSA_HEADS].reshape(nbs, DSA_HEADS, 1),
                         stail[:, :IDX_DIM].reshape(nbs, 1, IDX_DIM), cache_kidx[0])
    n_sel = min(TOPK_MAX, (past + 1) // 4)
    idx = _topk_idx(scores.reshape(nbs, past + page), n_sel)
    sdsa_o = _dsa_sample(idx, page_table, sdq, rel_bias,
                         cache_k[0].reshape(-1, 512), cache_v[0].reshape(-1, 512), sdk, sdv, page)
    ys = _layer_tail(sret_y, sdsa_o.reshape(nbs, 512), hs, wts, (nbs, nbs, 64, nbs, 2048))

    kv_p = lambda a: a.reshape(1, nbp, t, DSA_HEADS, DSA_DH)
    kv_s = lambda a: a.reshape(1, nbs, 1, DSA_HEADS, DSA_DH)
    return (y_prompt, ys.reshape(nbs, 1, D_MODEL), kv_p(dk), kv_p(dv),
            tail[:, :IDX_DIM].reshape(1, nbp, t, IDX_DIM), ret_s[None],
            kv_s(sdk), kv_s(sdv), stail[:, :IDX_DIM].reshape(1, nbs, 1, IDX_DIM), sret_s[None])
```

```python
import functools
import math

import numpy as np
import jax
import jax.numpy as jnp
from jax import lax
from jax.experimental import pallas as pl
from jax.experimental.pallas import tpu as pltpu

F32 = jnp.float32
BF16 = jnp.bfloat16
I32 = jnp.int32

D_MODEL = 1024
N_META = 16
RET_HEADS = 4
RET_DK = 128
DSA_HEADS = 8
DSA_DH = 64
DSA_W = DSA_HEADS * DSA_DH
IDX_DIM = 64
TOPK_MAX = 256
REL_BUCKETS = 32
REL_MAX_EXACT = 16
REL_MAX_DIST = 128
PEER_HEADS = 8
PEER_NKEYS = 128
PEER_DKEY = 256
PEER_TOPK = 16
HK = PEER_HEADS * PEER_TOPK
DN_ALPHA = 2.0 ** 0.25
LN_EPS = 1e-5
PROJ_MAIN = 4096
PROJ_TAIL = 128

LANES = 128
INT_MIN = -2 ** 31
VMEM_LIMIT = 56 * 1024 * 1024

NT = (((1,), (1,)), ((), ()))


def _params(sem, vmem=VMEM_LIMIT):
    return pltpu.CompilerParams(dimension_semantics=sem, vmem_limit_bytes=vmem)


def _ln(x):
    mu = jnp.mean(x, axis=-1, keepdims=True)
    xc = x - mu
    var = jnp.mean(xc * xc, axis=-1, keepdims=True)
    return xc * lax.rsqrt(var + LN_EPS)


def _inproj_kernel(x_ref, wa_ref, wt_ref, *out_refs):
    x = x_ref[...].astype(BF16)
    for j in range(8):
        out_refs[j][...] = jnp.dot(x, wa_ref[:, j * 512:(j + 1) * 512], preferred_element_type=F32)
    out_refs[8][...] = jnp.dot(x, wt_ref[...], preferred_element_type=F32)


def _inproj(x, wa, wt, tm):
    n = x.shape[0]
    row = lambda i: (i, 0)
    fixed = lambda i: (0, 0)
    return pl.pallas_call(
        _inproj_kernel,
        grid=(n // tm,),
        in_specs=[pl.BlockSpec((tm, D_MODEL), row), pl.BlockSpec((D_MODEL, PROJ_MAIN), fixed),
                  pl.BlockSpec((D_MODEL, PROJ_TAIL), fixed)],
        out_specs=[pl.BlockSpec((tm, 512), row)] * 8 + [pl.BlockSpec((tm, PROJ_TAIL), row)],
        out_shape=[jax.ShapeDtypeStruct((n, 512), F32)] * 8 + [jax.ShapeDtypeStruct((n, PROJ_TAIL), F32)],
        compiler_params=_params(("parallel",)),
        name="inproj",
    )(x, wa, wt)


def _ret_gammas():
    return [float(np.exp(np.log(np.float32(1.0 - 2.0 ** (-5.0 - h))))) for h in range(RET_HEADS)]


def _ret_tables(last_rows):
    c = LANES
    lg = np.log(1.0 - 2.0 ** (-5.0 - np.arange(RET_HEADS, dtype=np.float64)))
    i = np.arange(c, dtype=np.float64)
    diff = i[:, None] - i[None, :]
    dmask = np.where(diff[None] >= 0, np.exp(np.maximum(diff[None], 0.0) * lg[:, None, None]), 0.0)
    cdec = np.exp((i[None, :] + 1.0) * lg[:, None])[:, :, None] * np.ones((1, 1, c))
    kfull = np.exp((c - 1.0 - i)[None, :] * lg[:, None])
    klast = np.where(i[None, :] < last_rows, np.exp(np.maximum(last_rows - 1.0 - i, 0.0)[None, :] * lg[:, None]), 0.0)
    kdec = np.stack([kfull, klast])[:, :, :, None] * np.ones((1, 1, 1, c))
    g_full = [float(np.exp(c * l)) for l in lg]
    g_last = [float(np.exp(last_rows * l)) for l in lg]
    return (jnp.asarray(dmask, F32), jnp.asarray(cdec, F32), jnp.asarray(kdec, F32), g_full, g_last)


def _rot_tables(pos):
    half = RET_DK // 2
    inv = 1.0 / (10000.0 ** (jnp.arange(half, dtype=F32) / half))
    ang = pos.astype(F32)[:, None] * inv[None, :]
    cos, sin = jnp.cos(ang), jnp.sin(ang)
    return jnp.concatenate([cos, cos], -1), jnp.concatenate([-sin, sin], -1)


def _ret_prompt_kernel(q_ref, k_ref, v_ref, g_ref, cos_ref, sin_ref, dmask_ref, cdec_ref, kdec_ref,
                       y_ref, s_out_ref, s_scr, *, t_len, g_full, g_last):
    c = pl.program_id(1)
    is_last = c == pl.num_programs(1) - 1

    @pl.when(c == 0)
    def _():
        s_scr[...] = jnp.zeros_like(s_scr)

    row = lax.broadcasted_iota(I32, (LANES, 1), 0) + c * LANES
    valid = row < t_len
    cosf = cos_ref[...]
    sins = sin_ref[...]
    scale = RET_DK ** -0.5
    for h in range(RET_HEADS):
        sl = slice(h * RET_DK, (h + 1) * RET_DK)
        q = q_ref[:, sl]
        k = k_ref[:, sl]
        qr = jnp.where(valid, (q * cosf + pltpu.roll(q, RET_DK // 2, 1) * sins) * scale, 0.0)
        kr = jnp.where(valid, k * cosf + pltpu.roll(k, RET_DK // 2, 1) * sins, 0.0)
        v = jnp.where(valid, v_ref[:, sl], 0.0)
        qb = qr.astype(BF16)
        kb = kr.astype(BF16)
        vb = v.astype(BF16)
        sc = lax.dot_general(qb, kb, NT, preferred_element_type=F32) * dmask_ref[h]
        inner = jnp.dot(sc.astype(BF16), vb, preferred_element_type=F32)
        s_old = s_scr[h]
        cross = jnp.dot(qb, s_old.astype(BF16), preferred_element_type=F32) * cdec_ref[h]
        o = inner + cross
        kd_t = (kr * kdec_ref[h]).T.astype(BF16)
        gdec = jnp.where(is_last, g_last[h], g_full[h])
        s_scr[h] = gdec * s_old + jnp.dot(kd_t, vb, preferred_element_type=F32)
        g = g_ref[:, sl]
        y_ref[:, sl] = (_ln(o) * (g * jax.nn.sigmoid(g))).astype(BF16)

    @pl.when(is_last)
    def _():
        s_out_ref[...] = s_scr[...]


def _ret_prompt(rq, rk, rv, rg, cosf, sins):
    b, t, _ = rq.shape
    nc = pl.cdiv(t, LANES)
    last_rows = t - (nc - 1) * LANES
    dmask, cdec, kdec, g_full, g_last = _ret_tables(last_rows)
    blk = pl.BlockSpec((None, LANES, 512), lambda bi, ci: (bi, ci, 0))
    tab = pl.BlockSpec((LANES, RET_DK), lambda bi, ci: (ci, 0))
    full3 = pl.BlockSpec((RET_HEADS, LANES, LANES), lambda bi, ci: (0, 0, 0))
    return pl.pallas_call(
        functools.partial(_ret_prompt_kernel, t_len=t, g_full=g_full, g_last=g_last),
        grid=(b, nc),
        in_specs=[blk, blk, blk, blk, tab, tab, full3, full3,
                  pl.BlockSpec((None, RET_HEADS, LANES, LANES), lambda bi, ci: (ci // (nc - 1), 0, 0, 0))],
        out_specs=[blk, pl.BlockSpec((None, RET_HEADS, RET_DK, RET_DK), lambda bi, ci: (bi, 0, 0, 0))],
        out_shape=[jax.ShapeDtypeStruct((b, t, 512), BF16),
                   jax.ShapeDtypeStruct((b, RET_HEADS, RET_DK, RET_DK), F32)],
        scratch_shapes=[pltpu.VMEM((RET_HEADS, RET_DK, RET_DK), F32)],
        compiler_params=_params(("parallel", "arbitrary")),
        name="ret_prompt",
    )(rq, rk, rv, rg, cosf, sins, dmask, cdec, kdec)


def _ret_sample_kernel(q_ref, k_ref, qt_ref, kt_ref, v_ref, g_ref, cos_ref, sin_ref, cost_ref, sint_ref,
                       s0_ref, y_ref, s1_ref, o_scr, *, gammas):
    h = pl.program_id(0)
    gamma = jnp.float32(gammas[0])
    for i in range(1, RET_HEADS):
        gamma = jnp.where(h == i, jnp.float32(gammas[i]), gamma)
    scale = RET_DK ** -0.5
    half = RET_DK // 2

    def rot_t(x):
        return x * cost_ref[...] + jnp.concatenate([x[half:], x[:half]], axis=0) * sint_ref[...]

    def rot(x):
        return x * cos_ref[...] + pltpu.roll(x, half, 1) * sin_ref[...]

    def r16(x):
        return x.astype(BF16).astype(F32)

    qt = r16(rot_t(qt_ref[...]) * scale)
    kt = r16(rot_t(kt_ref[...]))
    q = r16(rot(q_ref[...]) * scale)
    k = r16(rot(k_ref[...]))
    vb = r16(v_ref[...])
    qk = r16(jnp.sum(q * k, axis=1, keepdims=True))
    inner = qk * vb
    for b in range(q.shape[0]):
        s_old = s0_ref[b]
        qc = qt[:, b:b + 1]
        kc = kt[:, b:b + 1]
        o_scr[b:b + 1, :] = jnp.sum(qc * r16(s_old), axis=0, keepdims=True) * gamma
        s1_ref[b] = gamma * s_old + kc * vb[b:b + 1, :]
    o = inner + o_scr[...]
    g = g_ref[...]
    y_ref[...] = (_ln(o) * (g * jax.nn.sigmoid(g))).astype(BF16)


def _ret_sample(rq, rk, rv, rg, state, pos):
    nb = rq.shape[0]
    cosf, sins = _rot_tables(jnp.full((1,), pos, I32))
    cos_b = jnp.broadcast_to(cosf, (nb, RET_DK))
    sin_b = jnp.broadcast_to(sins, (nb, RET_DK))
    cos_t = jnp.broadcast_to(cosf.reshape(RET_DK, 1), (RET_DK, nb))
    sin_t = jnp.broadcast_to(sins.reshape(RET_DK, 1), (RET_DK, nb))
    qt = rq.reshape(nb, RET_HEADS, RET_DK).transpose(1, 2, 0)
    kt = rk.reshape(nb, RET_HEADS, RET_DK).transpose(1, 2, 0)
    col = pl.BlockSpec((nb, RET_DK), lambda h: (0, h))
    tr = pl.BlockSpec((None, RET_DK, nb), lambda h: (h, 0, 0))
    full = pl.BlockSpec((nb, RET_DK), lambda h: (0, 0))
    full_t = pl.BlockSpec((RET_DK, nb), lambda h: (0, 0))
    st = pl.BlockSpec((nb, None, RET_DK, RET_DK), lambda h: (0, h, 0, 0))
    return pl.pallas_call(
        functools.partial(_ret_sample_kernel, gammas=_ret_gammas()),
        grid=(RET_HEADS,),
        in_specs=[col, col, tr, tr, col, col, full, full, full_t, full_t, st],
        out_specs=[col, st],
        out_shape=[jax.ShapeDtypeStruct((nb, 512), BF16), jax.ShapeDtypeStruct(state.shape, F32)],
        scratch_shapes=[pltpu.VMEM((nb, RET_DK), F32)],
        compiler_params=_params(("parallel",)),
        name="ret_sample",
    )(rq, rk, qt, kt, rv, rg, cos_b, sin_b, cos_t, sin_t, state)


def _bucket_np(d):
    d = np.maximum(d, 0)
    lb = REL_MAX_EXACT + (np.log(np.maximum(d, 1).astype(np.float32) / np.float32(REL_MAX_EXACT))
                          / np.float32(math.log(REL_MAX_DIST / REL_MAX_EXACT))
                          * np.float32(REL_BUCKETS - REL_MAX_EXACT)).astype(np.int32)
    return np.where(d < REL_MAX_EXACT, d, np.minimum(lb, REL_BUCKETS - 1)).astype(np.int32)


def _bias_kernel(rb_ref, bidx_ref, bidx_s_ref, out_ref, out_s_ref):
    for m in range(2):
        bi = bidx_ref[m]
        for h in range(DSA_HEADS):
            acc = jnp.zeros((LANES, LANES), F32)
            for b in range(REL_BUCKETS):
                acc = jnp.where(bi == b, rb_ref[b, h], acc)
            out_ref[h, m] = acc
    for m in range(3):
        bi = bidx_s_ref[m:m + 1, :]
        for h in range(DSA_HEADS):
            acc = jnp.zeros((1, LANES), F32)
            for b in range(REL_BUCKETS):
                acc = jnp.where(bi == b, rb_ref[b, h], acc)
            out_s_ref[m, h:h + 1, :] = acc


def _bias_tables(rel_bias):
    r = np.arange(LANES)
    d = r[:, None] - r[None, :]
    bidx = jnp.asarray(np.stack([_bucket_np(d), _bucket_np(d + LANES)]), I32)
    assert int(_bucket_np(np.array([LANES]))[0]) == REL_BUCKETS - 1
    bidx_s = jnp.asarray(np.stack([np.full(LANES, REL_BUCKETS - 1), _bucket_np(LANES - r), np.zeros(LANES)]), I32)
    return pl.pallas_call(
        _bias_kernel,
        in_specs=[pl.BlockSpec(memory_space=pltpu.SMEM), pl.BlockSpec(memory_space=pltpu.VMEM),
                  pl.BlockSpec(memory_space=pltpu.VMEM)],
        out_specs=[pl.BlockSpec(memory_space=pltpu.VMEM), pl.BlockSpec(memory_space=pltpu.VMEM)],
        out_shape=[jax.ShapeDtypeStruct((DSA_HEADS, 2, LANES, LANES), F32),
                   jax.ShapeDtypeStruct((3, DSA_HEADS, LANES), F32)],
        name="bias_tables",
    )(rel_bias, bidx, bidx_s)


def _order_key(x):
    bits = pltpu.bitcast(jnp.where(x == 0.0, 0.0, x), I32)
    return jnp.where(bits < 0, bits ^ jnp.int32(0x7FFFFFFF), bits)


def _count(mask):
    return jnp.sum(jnp.where(mask, 1.0, 0.0), axis=1, keepdims=True)


def _select_mask(key_ref, j0_ref, n_sel, col_bits, row_ok=None):
    rows, width = key_ref.shape

    def search(it, res):
        cand = res | lax.shift_left(jnp.int32(1), 31 - it)
        cnt = _count(key_ref[...] >= (cand ^ INT_MIN))
        return jnp.where(cnt >= n_sel, cand, res)

    thr = lax.fori_loop(0, 32, search, jnp.zeros((rows, 1), I32)) ^ INT_MIN
    key = key_ref[...]
    need = n_sel - _count(key > thr)
    n_ge = jnp.where(thr == INT_MIN, 0.0, _count(key >= thr))
    if row_ok is not None:
        n_ge = jnp.where(row_ok, n_ge, 0.0)
    j0_ref[...] = jnp.full((rows, 1), width, I32)

    @pl.when(jnp.max(n_ge) > n_sel)
    def _():
        def tie_search(it, res):
            cand = res | lax.shift_left(jnp.int32(1), col_bits - 1 - it)
            col = lax.broadcasted_iota(I32, (rows, width), 1)
            before = jnp.sum(jnp.where(key_ref[...] == thr, jnp.where(col < cand, 1.0, 0.0), 0.0),
                             axis=1, keepdims=True)
            return jnp.where(before < need, cand, res)

        j0_ref[...] = lax.fori_loop(0, col_bits, tie_search, jnp.zeros((rows, 1), I32))

    col = lax.broadcasted_iota(I32, (rows, width), 1)
    ninf = jnp.float32(-jnp.inf)
    tie_ok = jnp.where(key == thr, jnp.where(col <= j0_ref[...], 0.0, ninf), ninf)
    return jnp.where(key > thr, 0.0, tie_ok)


def _dsa_prompt_tile(i, tkw, bfar_ref, dq_ref, iq_ref, tq_ref, bt_ref, o_ref,
                     kpad, vpad, kipad, key_scr, am_scr, lg_scr, j0_scr, *, t_len, n_sel, col_bits):
    key_v = key_scr.at[:, 0:tkw]
    am_v = am_scr.at[:, 0:tkw]
    lg_v = lg_scr.at[:, 0:tkw]

    rowpos = i * LANES + lax.broadcasted_iota(I32, (LANES, 1), 0)
    row_ok = rowpos < t_len

    qi = jnp.where(row_ok, iq_ref[...], 0.0).astype(BF16)
    w = jnp.where(row_ok, tq_ref[:, IDX_DIM:IDX_DIM + DSA_HEADS], 0.0)
    ki = kipad[0:tkw, 0:IDX_DIM]
    sc = None
    for h in range(DSA_HEADS):
        d = lax.dot_general(qi[:, h * IDX_DIM:(h + 1) * IDX_DIM], ki, NT, preferred_element_type=F32)
        t = w[:, h:h + 1] * jnp.maximum(d, 0.0)
        sc = t if sc is None else sc + t

    col = lax.broadcasted_iota(I32, (LANES, tkw), 1)
    vis = col <= rowpos
    key_v[...] = jnp.where(vis, _order_key(sc), INT_MIN)
    am_v[...] = jnp.where(vis, _select_mask(key_v, j0_scr, n_sel, col_bits, row_ok), -jnp.inf)

    q = jnp.where(row_ok, dq_ref[...], 0.0) * (DSA_DH ** -0.5)
    c0 = pl.multiple_of(i * LANES, LANES)
    c1 = pl.multiple_of(jnp.maximum(i - 1, 0) * LANES, LANES)
    for h in range(DSA_HEADS):
        hs = slice(h * DSA_DH, (h + 1) * DSA_DH)
        qh = q[:, hs].astype(BF16)
        lg_v[...] = (lax.dot_general(qh, kpad[0:tkw, hs], NT, preferred_element_type=F32)
                     + bfar_ref[h] + am_v[...])
        for m, cm in ((1, c1), (0, c0)):
            blk = lax.dot_general(qh, kpad[pl.ds(cm, LANES), hs], NT, preferred_element_type=F32)
            lg_scr[:, pl.ds(cm, LANES)] = blk + bt_ref[h, m] + am_scr[:, pl.ds(cm, LANES)]
        lg = lg_v[...]
        mx = jnp.max(lg, axis=1, keepdims=True)
        p = jnp.exp(lg - mx)
        den = jnp.sum(p, axis=1, keepdims=True)
        oh = jnp.dot(p.astype(BF16), vpad[0:tkw, hs], preferred_element_type=F32) / den
        o_ref[:, hs] = oh.astype(BF16)


def _dsa_prompt_kernel(bfar_ref, dq_ref, iq_ref, tq_ref, dk_ref, dv_ref, tk_ref, bt_ref, o_ref,
                       kpad, vpad, kipad, key_scr, am_scr, lg_scr, j0_scr, *, t_len, tk_pad, widths, n_sel, col_bits):
    i = pl.program_id(1)

    @pl.when(i == 0)
    def _():
        kpad[0:t_len] = dk_ref[...].astype(BF16)
        kpad[t_len:tk_pad] = jnp.zeros((tk_pad - t_len, DSA_W), BF16)
        vpad[0:t_len] = dv_ref[...].astype(BF16)
        vpad[t_len:tk_pad] = jnp.zeros((tk_pad - t_len, DSA_W), BF16)
        kipad[0:t_len] = tk_ref[...].astype(BF16)
        kipad[t_len:tk_pad] = jnp.zeros((tk_pad - t_len, PROJ_TAIL), BF16)

    lo = 0
    for nblk in widths:
        @pl.when(jnp.logical_and(i >= lo, i < nblk))
        def _(nblk=nblk):
            _dsa_prompt_tile(i, nblk * LANES, bfar_ref, dq_ref, iq_ref, tq_ref, bt_ref, o_ref,
                             kpad, vpad, kipad, key_scr, am_scr, lg_scr, j0_scr,
                             t_len=t_len, n_sel=n_sel, col_bits=col_bits)
        lo = nblk


def _dsa_prompt(dq, dk, dv, iq, tail, bias_tiles, bias_far):
    b, t, _ = dq.shape
    n_sel = min(TOPK_MAX, t // 4)
    nq = pl.cdiv(t, LANES)
    tk_pad = nq * LANES
    col_bits = max(1, int(math.ceil(math.log2(tk_pad + 1))))
    widths = sorted({int(math.ceil(nq * j / 4)) for j in range(1, 5)})
    qblk = lambda wd: pl.BlockSpec((None, LANES, wd), lambda bi, qi: (bi, qi, 0))
    kblk = lambda wd: pl.BlockSpec((None, t, wd), lambda bi, qi: (bi, 0, 0))
    return pl.pallas_call(
        functools.partial(_dsa_prompt_kernel, t_len=t, tk_pad=tk_pad, widths=widths, n_sel=n_sel, col_bits=col_bits),
        grid=(b, nq),
        in_specs=[pl.BlockSpec(memory_space=pltpu.SMEM), qblk(DSA_W), qblk(DSA_W), qblk(PROJ_TAIL),
                  kblk(DSA_W), kblk(DSA_W), kblk(PROJ_TAIL),
                  pl.BlockSpec((DSA_HEADS, 2, LANES, LANES), lambda bi, qi: (0, 0, 0, 0))],
        out_specs=qblk(DSA_W),
        out_shape=jax.ShapeDtypeStruct((b, t, DSA_W), BF16),
        scratch_shapes=[pltpu.VMEM((tk_pad, DSA_W), BF16), pltpu.VMEM((tk_pad, DSA_W), BF16),
                        pltpu.VMEM((tk_pad, PROJ_TAIL), BF16), pltpu.VMEM((LANES, tk_pad), I32),
                        pltpu.VMEM((LANES, tk_pad), F32), pltpu.VMEM((LANES, tk_pad), F32),
                        pltpu.VMEM((LANES, 1), I32)],
        compiler_params=_params(("parallel", "arbitrary")),
        name="dsa_prompt",
    )(bias_far, dq, iq, tail, dk, dv, tail, bias_tiles)


def _idx_scores_kernel(pt_ref, qi_ref, w_ref, kin_ref, kidx_hbm, out_ref, buf, sem, *, n_pages, page, past):
    b = pl.program_id(0)

    def page_copy(p, src_page):
        dst = buf.at[:, pl.ds(pl.multiple_of(p * page, page), page)]
        return pltpu.make_async_copy(kidx_hbm.at[src_page], dst, sem)

    def start(p, c):
        page_copy(p, pt_ref[b, p]).start()
        return c

    lax.fori_loop(0, n_pages, start, 0)

    def wait(p, c):
        page_copy(p, 0).wait()
        return c

    lax.fori_loop(0, n_pages, wait, 0)
    qi = qi_ref[...].astype(BF16)
    w = w_ref[...]
    d = jnp.dot(qi, buf[...].astype(BF16), preferred_element_type=F32)
    out_ref[:, 0:past] = jnp.sum(w * jnp.maximum(d, 0.0), axis=0, keepdims=True)
    kn = kin_ref[...].astype(BF16).astype(F32)
    dn = jnp.sum(qi.astype(F32) * kn, axis=1, keepdims=True)
    sn = jnp.sum(w * jnp.maximum(dn, 0.0), axis=0, keepdims=True)
    lane = lax.broadcasted_iota(I32, (1, page), 1)
    out_ref[:, past:past + page] = jnp.where(lane == 0, sn, -jnp.inf)


def _idx_scores(page_table, qi, w, ki_new, kidx_t):
    nb, n_pages = page_table.shape
    page = kidx_t.shape[2]
    past = n_pages * page
    width = past + page
    grid_spec = pltpu.PrefetchScalarGridSpec(
        num_scalar_prefetch=1,
        grid=(nb,),
        in_specs=[pl.BlockSpec((None, DSA_HEADS, IDX_DIM), lambda b, pt: (b, 0, 0)),
                  pl.BlockSpec((None, DSA_HEADS, 1), lambda b, pt: (b, 0, 0)),
                  pl.BlockSpec((None, 1, IDX_DIM), lambda b, pt: (b, 0, 0)),
                  pl.BlockSpec(memory_space=pl.ANY)],
        out_specs=pl.BlockSpec((None, 1, width), lambda b, pt: (b, 0, 0)),
        scratch_shapes=[pltpu.VMEM((IDX_DIM, past), F32), pltpu.SemaphoreType.DMA(())],
    )
    return pl.pallas_call(
        functools.partial(_idx_scores_kernel, n_pages=n_pages, page=page, past=past),
        grid_spec=grid_spec,
        out_shape=jax.ShapeDtypeStruct((nb, 1, width), F32),
        compiler_params=_params(("arbitrary",)),
        name="idx_scores",
    )(page_table, qi, w, ki_new, kidx_t)


def _sel_mask_kernel(s_ref, am_ref, key_scr, j0_scr, *, n_sel, col_bits):
    key_scr[...] = _order_key(s_ref[...])
    am_ref[...] = _select_mask(key_scr, j0_scr, n_sel, col_bits)


def _sel_mask(scores, n_sel):
    nb, width = scores.shape
    col_bits = max(1, int(math.ceil(math.log2(width + 1))))
    return pl.pallas_call(
        functools.partial(_sel_mask_kernel, n_sel=n_sel, col_bits=col_bits),
        in_specs=[pl.BlockSpec(memory_space=pltpu.VMEM)],
        out_specs=pl.BlockSpec(memory_space=pltpu.VMEM),
        out_shape=jax.ShapeDtypeStruct((nb, width), F32),
        scratch_shapes=[pltpu.VMEM((nb, width), I32), pltpu.VMEM((nb, 1), I32)],
        compiler_params=pltpu.CompilerParams(vmem_limit_bytes=VMEM_LIMIT),
        name="sel_mask",
    )(scores)


PAGES_PER_STEP = 16


def _dsa_sample_kernel(pt_sm, q_ref, kn_ref, vn_ref, am_ref, amn_ref, bs_ref, ck_hbm, cv_hbm, o_ref,
                       kbuf, vbuf, sem, qb_scr, m_scr, l_scr, acc_scr):
    pps = PAGES_PER_STEP
    c = pl.program_id(1)
    n_chunks = pl.num_programs(1)
    last = n_chunks - 1
    step = pl.program_id(0) * n_chunks + c
    n_steps = pl.num_programs(0) * n_chunks
    slot = step % 2

    def chunk_copies(st, sl, lookup):
        cps = []
        for i in range(pps):
            pg = pt_sm[st // n_chunks, (st % n_chunks) * pps + i] if lookup else 0
            cps.append(pltpu.make_async_copy(ck_hbm.at[pg], kbuf.at[sl, i], sem.at[0, sl]))
            cps.append(pltpu.make_async_copy(cv_hbm.at[pg], vbuf.at[sl, i], sem.at[1, sl]))
        return cps

    @pl.when(step == 0)
    def _():
        for cp in chunk_copies(step, slot, True):
            cp.start()

    @pl.when(step + 1 < n_steps)
    def _():
        for cp in chunk_copies(step + 1, 1 - slot, True):
            cp.start()

    for cp in chunk_copies(step, slot, False):
        cp.wait()

    scale = DSA_DH ** -0.5
    eye = (lax.broadcasted_iota(I32, (DSA_DH, DSA_DH), 0) == lax.broadcasted_iota(I32, (DSA_DH, DSA_DH), 1))
    hrows = lambda h: slice(h * DSA_DH, (h + 1) * DSA_DH)

    @pl.when(c == 0)
    def _():
        q = q_ref[...] * scale
        for h in range(DSA_HEADS):
            qrow = jnp.broadcast_to(q[:, hrows(h)], (DSA_DH, DSA_DH))
            qcol = jnp.sum(jnp.where(eye, qrow, 0.0), axis=1, keepdims=True)
            qb_scr[hrows(h), :] = jnp.broadcast_to(qcol, (DSA_DH, LANES))
        m_scr[...] = jnp.full(m_scr.shape, -1e30, F32)
        l_scr[...] = jnp.zeros_like(l_scr)
        acc_scr[...] = jnp.zeros_like(acc_scr)

    qb = qb_scr[...]
    lgs = []
    for i in range(pps):
        prod = kbuf[slot, i].reshape(DSA_W, LANES) * qb
        lg_i = jnp.concatenate([jnp.sum(prod[hrows(h)], axis=0, keepdims=True) for h in range(DSA_HEADS)], axis=0)
        bias = bs_ref[0]
        if i == pps - 1:
            bias = jnp.where(c == last, bs_ref[1], bias)
        lgs.append(lg_i + bias)
    lg = jnp.concatenate(lgs, axis=1) + am_ref[...]
    m_old = m_scr[:, 0:1]
    m_new = jnp.maximum(m_old, jnp.max(lg, axis=1, keepdims=True))
    alpha = jnp.exp(m_old - m_new)
    p = jnp.exp(lg - m_new)
    l_scr[...] = jnp.broadcast_to(l_scr[:, 0:1] * alpha + jnp.sum(p, axis=1, keepdims=True), l_scr.shape)
    m_scr[...] = jnp.broadcast_to(m_new, m_scr.shape)
    for h in range(DSA_HEADS):
        a = acc_scr[hrows(h), :] * alpha[h:h + 1, :]
        for i in range(pps):
            a = a + vbuf[slot, i, h] * p[h:h + 1, i * LANES:(i + 1) * LANES]
        acc_scr[hrows(h), :] = a

    @pl.when(c == last)
    def _():
        q = q_ref[...] * scale
        kn = kn_ref[...]
        vn = vn_ref[...]
        outs = []
        for h in range(DSA_HEADS):
            lgn = (jnp.sum(q[:, hrows(h)] * kn[:, hrows(h)], axis=1, keepdims=True)
                   + bs_ref[2][h:h + 1, 0:1] + amn_ref[:, 0:1])
            mo = m_scr[h:h + 1, 0:1]
            mn = jnp.maximum(mo, lgn)
            al = jnp.exp(mo - mn)
            pn = jnp.exp(lgn - mn)
            den = l_scr[h:h + 1, 0:1] * al + pn
            ocol = jnp.sum(acc_scr[hrows(h), :], axis=1, keepdims=True)
            orow = jnp.sum(jnp.where(eye, jnp.broadcast_to(ocol, (DSA_DH, DSA_DH)), 0.0), axis=0, keepdims=True)
            outs.append((orow * al + pn * vn[:, hrows(h)]) / den)
        o_ref[...] = jnp.concatenate(outs, axis=1).astype(BF16)


def _dsa_sample(page_table, dq, dk_new, dv_new, am, bias_rows, ck_t, cv_t):
    nb, n_pages = page_table.shape
    page = ck_t.shape[3]
    pps = PAGES_PER_STEP
    n_chunks = n_pages // pps
    row = lambda wd: pl.BlockSpec((None, 1, wd), lambda b, c, pt: (b, 0, 0))
    grid_spec = pltpu.PrefetchScalarGridSpec(
        num_scalar_prefetch=1,
        grid=(nb, n_chunks),
        in_specs=[row(DSA_W), row(DSA_W), row(DSA_W),
                  pl.BlockSpec((None, 1, pps * page), lambda b, c, pt: (b, 0, c)),
                  pl.BlockSpec((None, 1, page), lambda b, c, pt: (b, 0, n_pages)),
                  pl.BlockSpec((3, DSA_HEADS, LANES), lambda b, c, pt: (0, 0, 0)),
                  pl.BlockSpec(memory_space=pl.ANY), pl.BlockSpec(memory_space=pl.ANY)],
        out_specs=row(DSA_W),
        scratch_shapes=[pltpu.VMEM((2, pps, DSA_HEADS, DSA_DH, page), F32),
                        pltpu.VMEM((2, pps, DSA_HEADS, DSA_DH, page), F32),
                        pltpu.SemaphoreType.DMA((2, 2)),
                        pltpu.VMEM((DSA_W, LANES), F32), pltpu.VMEM((DSA_HEADS, LANES), F32),
                        pltpu.VMEM((DSA_HEADS, LANES), F32), pltpu.VMEM((DSA_W, LANES), F32)],
    )
    r3 = lambda a: a.reshape(nb, 1, a.shape[-1])
    return pl.pallas_call(
        _dsa_sample_kernel,
        grid_spec=grid_spec,
        out_shape=jax.ShapeDtypeStruct((nb, 1, DSA_W), BF16),
        compiler_params=_params(("arbitrary", "arbitrary")),
        name="dsa_sample",
    )(page_table, r3(dq), r3(dk_new), r3(dv_new), r3(am), r3(am), bias_rows, ck_t, cv_t)


def _tail1_kernel(ry_ref, do_ref, h_ref, wo_ref, g1_ref, b1_ref, wq_ref, sk_ref, h1_ref, h1b_ref, st_ref):
    half = wo_ref.shape[0] // 2
    mix = (jnp.dot(ry_ref[...], wo_ref[0:half], preferred_element_type=F32)
           + jnp.dot(do_ref[...], wo_ref[half:], preferred_element_type=F32))
    h1 = _ln(DN_ALPHA * h_ref[...] + mix) * g1_ref[...] + b1_ref[...]
    h1_ref[...] = h1
    h1b = h1.astype(BF16)
    h1b_ref[...] = h1b
    q = jnp.dot(h1b, wq_ref[...], preferred_element_type=F32)
    for hh in range(PEER_HEADS):
        qh = _ln(q[:, hh * PEER_DKEY:(hh + 1) * PEER_DKEY]).astype(BF16)
        for s in range(2):
            qs = qh[:, s * (PEER_DKEY // 2):(s + 1) * (PEER_DKEY // 2)]
            st_ref[hh * 2 + s] = lax.dot_general(sk_ref[hh, s], qs, NT, preferred_element_type=F32)


def _tail1(ret_y, dsa_o, h, wo, g1, b1, wq, sk, tm):
    n = h.shape[0]
    row = lambda wd: pl.BlockSpec((tm, wd), lambda i: (i, 0))
    fixed = lambda shp: pl.BlockSpec(shp, lambda i: (0,) * len(shp))
    return pl.pallas_call(
        _tail1_kernel,
        grid=(n // tm,),
        in_specs=[row(512), row(512), row(D_MODEL), fixed(wo.shape), fixed((1, D_MODEL)), fixed((1, D_MODEL)),
                  fixed(wq.shape), fixed(sk.shape)],
        out_specs=[row(D_MODEL), row(D_MODEL),
                   pl.BlockSpec((2 * PEER_HEADS, PEER_NKEYS, tm), lambda i: (0, 0, i))],
        out_shape=[jax.ShapeDtypeStruct((n, D_MODEL), F32), jax.ShapeDtypeStruct((n, D_MODEL), BF16),
                   jax.ShapeDtypeStruct((2 * PEER_HEADS, PEER_NKEYS, n), F32)],
        compiler_params=_params(("parallel",)),
        name="tail1",
    )(ret_y, dsa_o, h, wo, g1, b1, wq, sk)


def _top16(s):
    kio = lax.broadcasted_iota(I32, s.shape, 0).astype(F32)
    vals, idxs = [], []
    for _ in range(PEER_TOPK):
        m = jnp.max(s, axis=0, keepdims=True)
        first = jnp.min(jnp.where(s == m, kio, float(s.shape[0])), axis=0, keepdims=True)
        s = jnp.where(kio == first, -jnp.inf, s)
        vals.append(m)
        idxs.append(first)
    return jnp.concatenate(vals, 0), jnp.concatenate(idxs, 0)


def _route_kernel(st_ref, g_ref, e_ref):
    r8 = lax.broadcasted_iota(I32, (8, LANES), 0).astype(F32)

    def head(hh, c):
        v1, i1 = _top16(st_ref[2 * hh])
        v2, i2 = _top16(st_ref[2 * hh + 1])
        vals = [v1[0:8] + v2[0:1], v1[8:16] + v2[0:1]]
        flat = [r8 * 16.0, (r8 + 8.0) * 16.0]
        exp_id = [i1[0:8] * PEER_NKEYS + i2[0:1], i1[8:16] * PEER_NKEYS + i2[0:1]]
        for j in range(1, 8):
            vals.append(v1[0:8] + v2[j:j + 1])
            flat.append(r8 * 16.0 + float(j))
            exp_id.append(i1[0:8] * PEER_NKEYS + i2[j:j + 1])
        vals.append(v1[0:1] + v2[8:16])
        flat.append(r8 + 8.0)
        exp_id.append(i1[0:1] * PEER_NKEYS + i2[8:16])
        cand = jnp.concatenate(vals, 0)
        pos = jnp.concatenate(flat, 0)
        eid = jnp.concatenate(exp_id, 0)
        tops, es = [], []
        for _ in range(PEER_TOPK):
            m = jnp.max(cand, axis=0, keepdims=True)
            pm = jnp.min(jnp.where(cand == m, pos, 1e9), axis=0, keepdims=True)
            hit = pos == pm
            es.append(jnp.max(jnp.where(hit, eid, -1.0), axis=0, keepdims=True))
            cand = jnp.where(hit, -jnp.inf, cand)
            tops.append(m)
        top = jnp.concatenate(tops, 0)
        ex = jnp.exp(top - top[0:1])
        off = pl.multiple_of(hh * PEER_TOPK, PEER_TOPK)
        g_ref[pl.ds(off, PEER_TOPK), :] = ex / jnp.sum(ex, axis=0, keepdims=True)
        e_ref[pl.ds(off, PEER_TOPK), :] = jnp.concatenate(es, 0).astype(I32)
        return c

    lax.fori_loop(0, PEER_HEADS, head, 0)


def _route(st):
    n = st.shape[2]
    blk = pl.BlockSpec((HK, LANES), lambda i: (0, i))
    return pl.pallas_call(
        _route_kernel,
        grid=(n // LANES,),
        in_specs=[pl.BlockSpec((2 * PEER_HEADS, PEER_NKEYS, LANES), lambda i: (0, 0, i))],
        out_specs=[blk, blk],
        out_shape=[jax.ShapeDtypeStruct((HK, n), F32), jax.ShapeDtypeStruct((HK, n), I32)],
        compiler_params=_params(("parallel",)),
        name="route",
    )(st)


EXPERT_BLOCK = 256


def _peer_act_kernel(x_ref, u_ref, e_ref, g_ref, w_ref, acc_scr):
    j = pl.program_id(1)

    @pl.when(j == 0)
    def _():
        acc_scr[...] = jnp.zeros_like(acc_scr)

    hmat = lax.dot_general(x_ref[...], u_ref[...], NT, preferred_element_type=F32)
    e = e_ref[...]
    i2 = e & (PEER_NKEYS - 1)
    i1 = e >> 7
    acc = acc_scr[...]
    for s in range(EXPERT_BLOCK // PEER_NKEYS):
        got = jnp.take_along_axis(hmat[:, s * PEER_NKEYS:(s + 1) * PEER_NKEYS], i2, axis=1)
        acc = jnp.where(i1 == j * (EXPERT_BLOCK // PEER_NKEYS) + s, got, acc)
    acc_scr[...] = acc

    @pl.when(j == pl.num_programs(1) - 1)
    def _():
        a = acc_scr[...]
        gelu = 0.5 * a * (1.0 + lax.erf(a * (2.0 ** -0.5)))
        w_ref[...] = g_ref[...] * gelu


def _peer_act(xb, u, e, g, tr):
    n = xb.shape[0]
    row = lambda wd: pl.BlockSpec((tr, wd), lambda i, j: (i, 0))
    return pl.pallas_call(
        _peer_act_kernel,
        grid=(n // tr, u.shape[0] // EXPERT_BLOCK),
        in_specs=[row(D_MODEL), pl.BlockSpec((EXPERT_BLOCK, D_MODEL), lambda i, j: (j, 0)), row(HK), row(HK)],
        out_specs=row(HK),
        out_shape=jax.ShapeDtypeStruct((n, HK), F32),
        scratch_shapes=[pltpu.VMEM((tr, HK), F32)],
        compiler_params=_params(("parallel", "arbitrary")),
        name="peer_act",
    )(xb, u, e, g)


GATE_UNROLL = 8


def _peer_out_kernel(e_ref, w_ref, v_ref, h1_ref, g2_ref, b2_ref, y_ref, p_scr, acc_scr):
    k = pl.program_id(1)
    tr = e_ref.shape[0]
    rows_per_step = v_ref.shape[0] // PEER_NKEYS

    @pl.when(k == 0)
    def _():
        acc_scr[...] = jnp.zeros_like(acc_scr)
        sub = lax.broadcasted_iota(I32, (PEER_NKEYS, HK), 0)

        def body(r, c):
            er = e_ref[pl.ds(r, 1), :]
            wr = w_ref[pl.ds(r, 1), :]
            o1 = jnp.where(sub == (er >> 7), 1.0, 0.0).astype(BF16)
            o2 = jnp.where(sub == (er & (PEER_NKEYS - 1)), wr, 0.0).astype(BF16)
            row0 = pl.multiple_of(r * PEER_NKEYS, PEER_NKEYS)
            p_scr[pl.ds(row0, PEER_NKEYS), :] = lax.dot_general(o1, o2, NT, preferred_element_type=F32)
            return c

        lax.fori_loop(0, tr, body, 0, unroll=GATE_UNROLL)

    tot = None
    for j in range(0, rows_per_step, 2):
        i1 = k * rows_per_step + j
        lhs = jnp.concatenate([p_scr[pl.ds(i1, tr, stride=PEER_NKEYS), :],
                               p_scr[pl.ds(i1 + 1, tr, stride=PEER_NKEYS), :]], axis=1).astype(BF16)
        d = jnp.dot(lhs, v_ref[j * PEER_NKEYS:(j + 2) * PEER_NKEYS, :], preferred_element_type=F32)
        tot = d if tot is None else tot + d
    acc_scr[...] += tot

    @pl.when(k == pl.num_programs(1) - 1)
    def _():
        y_ref[...] = _ln(DN_ALPHA * h1_ref[...] + acc_scr[...]) * g2_ref[...] + b2_ref[...]


def _peer_out(e, w, v, h1, g2, b2, tr, kc):
    n = h1.shape[0]
    row = lambda wd: pl.BlockSpec((tr, wd), lambda i, k: (i, 0))
    vec = pl.BlockSpec((1, D_MODEL), lambda i, k: (0, 0))
    return pl.pallas_call(
        _peer_out_kernel,
        grid=(n // tr, v.shape[0] // kc),
        in_specs=[row(HK), row(HK), pl.BlockSpec((kc, D_MODEL), lambda i, k: (k, 0)), row(D_MODEL), vec, vec],
        out_specs=row(D_MODEL),
        out_shape=jax.ShapeDtypeStruct((n, D_MODEL), F32),
        scratch_shapes=[pltpu.VMEM((tr * PEER_NKEYS, PEER_NKEYS), F32), pltpu.VMEM((tr, D_MODEL), F32)],
        compiler_params=_params(("parallel", "arbitrary")),
        name="peer_out",
    )(e, w, v, h1, g2, b2)


def _layer_tail(ret_y, dsa_o, h, wts, tiles):
    wo, g1, b1, g2, b2, wq, sk, u, v = wts
    tm, tr_act, tr_out, kc = tiles
    h1, h1b, st = _tail1(ret_y, dsa_o, h, wo, g1, b1, wq, sk, tm)
    g_t, e_t = _route(st)
    g = g_t.T
    e = e_t.T
    w = _peer_act(h1b, u, e, g, tr_act)
    return _peer_out(e, w, v, h1, g2, b2, tr_out, kc)


def kernel(x_prompt, x_sample, cache_k, cache_v, cache_kidx, state_ret, page_table, meta_tokens, rel_bias,
           w_in, w_out, ln1_g, ln1_b, ln2_g, ln2_b, peer_wq, peer_subkeys, peer_u, peer_v):
    assert w_in.shape[0] == 1 and x_sample.shape[1] == 1
    nbp, seq, _ = x_prompt.shape
    t = seq + N_META
    nbs = x_sample.shape[0]
    n_pages = page_table.shape[1]
    page = cache_kidx.shape[2]
    past = n_pages * page
    assert page == LANES and n_pages % PAGES_PER_STEP == 0

    hp = jnp.concatenate([jnp.broadcast_to(meta_tokens[None], (nbp, N_META, D_MODEL)), x_prompt], 1)
    hp = hp.reshape(nbp * t, D_MODEL)
    hs = x_sample.reshape(nbs, D_MODEL)

    w = w_in[0]
    wa = w[:, :PROJ_MAIN].astype(BF16)
    wt = jnp.pad(w[:, PROJ_MAIN:], ((0, 0), (0, PROJ_TAIL - (w.shape[1] - PROJ_MAIN)))).astype(BF16)
    wts = (w_out[0].astype(BF16), ln1_g, ln1_b, ln2_g, ln2_b, peer_wq[0].astype(BF16),
           peer_subkeys[0].astype(BF16), peer_u[0].astype(BF16), peer_v[0].astype(BF16))
    bias_tiles, bias_rows = _bias_tables(rel_bias)

    rq, rk, rv, rg, dq, dk, dv, iq, tail = _inproj(hp, wa, wt, 384)
    b3 = lambda a: a.reshape(nbp, t, a.shape[-1])
    cosf, sins = _rot_tables(jnp.arange(t, dtype=I32))
    ret_y, ret_s = _ret_prompt(b3(rq), b3(rk), b3(rv), b3(rg), cosf, sins)
    dsa_o = _dsa_prompt(b3(dq), b3(dk), b3(dv), b3(iq), b3(tail), bias_tiles, rel_bias[REL_BUCKETS - 1])
    yp = _layer_tail(ret_y.reshape(nbp * t, 512), dsa_o.reshape(nbp * t, DSA_W), hp, wts,
                     (384, 1376, 384, 2048))
    y_prompt = yp.reshape(nbp, t, D_MODEL)[:, N_META:]

    kidx_t = jnp.transpose(cache_kidx[0], (0, 2, 1))
    ck_t = jnp.transpose(cache_k[0], (0, 2, 3, 1))
    cv_t = jnp.transpose(cache_v[0], (0, 2, 3, 1))
    srq, srk, srv, srg, sdq, sdk, sdv, siq, stail = _inproj(hs, wa, wt, nbs)
    sret_y, sret_s = _ret_sample(srq, srk, srv, srg, state_ret[0], past)
    scores = _idx_scores(page_table, siq.reshape(nbs, DSA_HEADS, IDX_DIM),
                         stail[:, IDX_DIM:IDX_DIM + DSA_HEADS].reshape(nbs, DSA_HEADS, 1),
                         stail[:, :IDX_DIM].reshape(nbs, 1, IDX_DIM), kidx_t)
    n_sel = min(TOPK_MAX, (past + 1) // 4)
    am = _sel_mask(scores.reshape(nbs, past + page), n_sel)
    sdsa_o = _dsa_sample(page_table, sdq, sdk, sdv, am, bias_rows, ck_t, cv_t)
    ys = _layer_tail(sret_y, sdsa_o.reshape(nbs, DSA_W), hs, wts, (nbs, nbs, nbs, 2048))

    kv_p = lambda a: a.reshape(1, nbp, t, DSA_HEADS, DSA_DH)
    kv_s = lambda a: a.reshape(1, nbs, 1, DSA_HEADS, DSA_DH)
    return (y_prompt, ys.reshape(nbs, 1, D_MODEL), kv_p(dk), kv_p(dv),
            tail[:, :IDX_DIM].reshape(1, nbp, t, IDX_DIM), ret_s[None],
            kv_s(sdk), kv_s(sdv), stail[:, :IDX_DIM].reshape(1, nbs, 1, IDX_DIM), sret_s[None])
```

```python
import functools
import math

import numpy as np
import jax
import jax.numpy as jnp
from jax import lax
from jax.experimental import pallas as pl
from jax.experimental.pallas import tpu as pltpu

F32 = jnp.float32
BF16 = jnp.bfloat16
I32 = jnp.int32

D_MODEL = 1024
N_META = 16
RET_HEADS = 4
RET_DK = 128
DSA_HEADS = 8
DSA_DH = 64
DSA_W = DSA_HEADS * DSA_DH
IDX_DIM = 64
TOPK_MAX = 256
REL_BUCKETS = 32
REL_MAX_EXACT = 16
REL_MAX_DIST = 128
PEER_HEADS = 8
PEER_NKEYS = 128
PEER_DKEY = 256
PEER_TOPK = 16
HK = PEER_HEADS * PEER_TOPK
DN_ALPHA = 2.0 ** 0.25
LN_EPS = 1e-5
PROJ_MAIN = 4096
PROJ_TAIL = 128

LANES = 128
INT_MIN = -2 ** 31
VMEM_LIMIT = 56 * 1024 * 1024

NT = (((1,), (1,)), ((), ()))


def _params(sem, vmem=VMEM_LIMIT):
    return pltpu.CompilerParams(dimension_semantics=sem, vmem_limit_bytes=vmem)


def _ln(x):
    mu = jnp.mean(x, axis=-1, keepdims=True)
    xc = x - mu
    var = jnp.mean(xc * xc, axis=-1, keepdims=True)
    return xc * lax.rsqrt(var + LN_EPS)


def _inproj_kernel(x_ref, wa_ref, wt_ref, *out_refs):
    x = x_ref[...].astype(BF16)
    for j in range(8):
        out_refs[j][...] = jnp.dot(x, wa_ref[:, j * 512:(j + 1) * 512], preferred_element_type=F32)
    out_refs[8][...] = jnp.dot(x, wt_ref[...], preferred_element_type=F32)


def _inproj(x, wa, wt, tm):
    n = x.shape[0]
    row = lambda i: (i, 0)
    fixed = lambda i: (0, 0)
    return pl.pallas_call(
        _inproj_kernel,
        grid=(n // tm,),
        in_specs=[pl.BlockSpec((tm, D_MODEL), row), pl.BlockSpec((D_MODEL, PROJ_MAIN), fixed),
                  pl.BlockSpec((D_MODEL, PROJ_TAIL), fixed)],
        out_specs=[pl.BlockSpec((tm, 512), row)] * 8 + [pl.BlockSpec((tm, PROJ_TAIL), row)],
        out_shape=[jax.ShapeDtypeStruct((n, 512), F32)] * 8 + [jax.ShapeDtypeStruct((n, PROJ_TAIL), F32)],
        compiler_params=_params(("parallel",)),
        name="inproj",
    )(x, wa, wt)


def _ret_gammas():
    return [float(np.exp(np.log(np.float32(1.0 - 2.0 ** (-5.0 - h))))) for h in range(RET_HEADS)]


def _ret_tables(last_rows):
    c = LANES
    lg = np.log(1.0 - 2.0 ** (-5.0 - np.arange(RET_HEADS, dtype=np.float64)))
    i = np.arange(c, dtype=np.float64)
    diff = i[:, None] - i[None, :]
    dmask = np.where(diff[None] >= 0, np.exp(np.maximum(diff[None], 0.0) * lg[:, None, None]), 0.0)
    cdec = np.exp((i[None, :] + 1.0) * lg[:, None])[:, :, None] * np.ones((1, 1, c))
    kfull = np.exp((c - 1.0 - i)[None, :] * lg[:, None])
    klast = np.where(i[None, :] < last_rows, np.exp(np.maximum(last_rows - 1.0 - i, 0.0)[None, :] * lg[:, None]), 0.0)
    kdec = np.stack([kfull, klast])[:, :, :, None] * np.ones((1, 1, 1, c))
    g_full = [float(np.exp(c * l)) for l in lg]
    g_last = [float(np.exp(last_rows * l)) for l in lg]
    return (jnp.asarray(dmask, F32), jnp.asarray(cdec, F32), jnp.asarray(kdec, F32), g_full, g_last)


def _rot_tables(pos):
    half = RET_DK // 2
    inv = 1.0 / (10000.0 ** (jnp.arange(half, dtype=F32) / half))
    ang = pos.astype(F32)[:, None] * inv[None, :]
    cos, sin = jnp.cos(ang), jnp.sin(ang)
    return jnp.concatenate([cos, cos], -1), jnp.concatenate([-sin, sin], -1)


def _ret_prompt_kernel(q_ref, k_ref, v_ref, g_ref, cos_ref, sin_ref, dmask_ref, cdec_ref, kdec_ref,
                       y_ref, s_out_ref, s_scr, *, t_len, g_full, g_last):
    c = pl.program_id(1)
    is_last = c == pl.num_programs(1) - 1

    @pl.when(c == 0)
    def _():
        s_scr[...] = jnp.zeros_like(s_scr)

    row = lax.broadcasted_iota(I32, (LANES, 1), 0) + c * LANES
    valid = row < t_len
    cosf = cos_ref[...]
    sins = sin_ref[...]
    scale = RET_DK ** -0.5
    for h in range(RET_HEADS):
        sl = slice(h * RET_DK, (h + 1) * RET_DK)
        q = q_ref[:, sl]
        k = k_ref[:, sl]
        qr = jnp.where(valid, (q * cosf + pltpu.roll(q, RET_DK // 2, 1) * sins) * scale, 0.0)
        kr = jnp.where(valid, k * cosf + pltpu.roll(k, RET_DK // 2, 1) * sins, 0.0)
        v = jnp.where(valid, v_ref[:, sl], 0.0)
        qb = qr.astype(BF16)
        kb = kr.astype(BF16)
        vb = v.astype(BF16)
        sc = lax.dot_general(qb, kb, NT, preferred_element_type=F32) * dmask_ref[h]
        inner = jnp.dot(sc.astype(BF16), vb, preferred_element_type=F32)
        s_old = s_scr[h]
        cross = jnp.dot(qb, s_old.astype(BF16), preferred_element_type=F32) * cdec_ref[h]
        o = inner + cross
        kd_t = (kr * kdec_ref[h]).T.astype(BF16)
        gdec = jnp.where(is_last, g_last[h], g_full[h])
        s_scr[h] = gdec * s_old + jnp.dot(kd_t, vb, preferred_element_type=F32)
        g = g_ref[:, sl]
        y_ref[:, sl] = (_ln(o) * (g * jax.nn.sigmoid(g))).astype(BF16)

    @pl.when(is_last)
    def _():
        s_out_ref[...] = s_scr[...]


def _ret_prompt(rq, rk, rv, rg, cosf, sins):
    b, t, _ = rq.shape
    nc = pl.cdiv(t, LANES)
    last_rows = t - (nc - 1) * LANES
    dmask, cdec, kdec, g_full, g_last = _ret_tables(last_rows)
    blk = pl.BlockSpec((None, LANES, 512), lambda bi, ci: (bi, ci, 0))
    tab = pl.BlockSpec((LANES, RET_DK), lambda bi, ci: (ci, 0))
    full3 = pl.BlockSpec((RET_HEADS, LANES, LANES), lambda bi, ci: (0, 0, 0))
    return pl.pallas_call(
        functools.partial(_ret_prompt_kernel, t_len=t, g_full=g_full, g_last=g_last),
        grid=(b, nc),
        in_specs=[blk, blk, blk, blk, tab, tab, full3, full3,
                  pl.BlockSpec((None, RET_HEADS, LANES, LANES), lambda bi, ci: (ci // (nc - 1), 0, 0, 0))],
        out_specs=[blk, pl.BlockSpec((None, RET_HEADS, RET_DK, RET_DK), lambda bi, ci: (bi, 0, 0, 0))],
        out_shape=[jax.ShapeDtypeStruct((b, t, 512), BF16),
                   jax.ShapeDtypeStruct((b, RET_HEADS, RET_DK, RET_DK), F32)],
        scratch_shapes=[pltpu.VMEM((RET_HEADS, RET_DK, RET_DK), F32)],
        compiler_params=_params(("parallel", "arbitrary")),
        name="ret_prompt",
    )(rq, rk, rv, rg, cosf, sins, dmask, cdec, kdec)


def _ret_sample_kernel(q_ref, k_ref, qt_ref, kt_ref, v_ref, g_ref, cos_ref, sin_ref, cost_ref, sint_ref,
                       s0_ref, y_ref, s1_ref, o_scr, *, gammas):
    h = pl.program_id(0)
    gamma = jnp.float32(gammas[0])
    for i in range(1, RET_HEADS):
        gamma = jnp.where(h == i, jnp.float32(gammas[i]), gamma)
    scale = RET_DK ** -0.5
    half = RET_DK // 2

    def rot_t(x):
        return x * cost_ref[...] + jnp.concatenate([x[half:], x[:half]], axis=0) * sint_ref[...]

    def rot(x):
        return x * cos_ref[...] + pltpu.roll(x, half, 1) * sin_ref[...]

    def r16(x):
        return x.astype(BF16).astype(F32)

    qt = r16(rot_t(qt_ref[...]) * scale)
    kt = r16(rot_t(kt_ref[...]))
    q = r16(rot(q_ref[...]) * scale)
    k = r16(rot(k_ref[...]))
    vb = r16(v_ref[...])
    qk = r16(jnp.sum(q * k, axis=1, keepdims=True))
    inner = qk * vb
    for b in range(q.shape[0]):
        s_old = s0_ref[b]
        qc = qt[:, b:b + 1]
        kc = kt[:, b:b + 1]
        o_scr[b:b + 1, :] = jnp.sum(qc * r16(s_old), axis=0, keepdims=True) * gamma
        s1_ref[b] = gamma * s_old + kc * vb[b:b + 1, :]
    o = inner + o_scr[...]
    g = g_ref[...]
    y_ref[...] = (_ln(o) * (g * jax.nn.sigmoid(g))).astype(BF16)


def _ret_sample(rq, rk, rv, rg, state, pos):
    nb = rq.shape[0]
    cosf, sins = _rot_tables(jnp.full((1,), pos, I32))
    cos_b = jnp.broadcast_to(cosf, (nb, RET_DK))
    sin_b = jnp.broadcast_to(sins, (nb, RET_DK))
    cos_t = jnp.broadcast_to(cosf.reshape(RET_DK, 1), (RET_DK, nb))
    sin_t = jnp.broadcast_to(sins.reshape(RET_DK, 1), (RET_DK, nb))
    qt = rq.reshape(nb, RET_HEADS, RET_DK).transpose(1, 2, 0)
    kt = rk.reshape(nb, RET_HEADS, RET_DK).transpose(1, 2, 0)
    col = pl.BlockSpec((nb, RET_DK), lambda h: (0, h))
    tr = pl.BlockSpec((None, RET_DK, nb), lambda h: (h, 0, 0))
    full = pl.BlockSpec((nb, RET_DK), lambda h: (0, 0))
    full_t = pl.BlockSpec((RET_DK, nb), lambda h: (0, 0))
    st = pl.BlockSpec((nb, None, RET_DK, RET_DK), lambda h: (0, h, 0, 0))
    return pl.pallas_call(
        functools.partial(_ret_sample_kernel, gammas=_ret_gammas()),
        grid=(RET_HEADS,),
        in_specs=[col, col, tr, tr, col, col, full, full, full_t, full_t, st],
        out_specs=[col, st],
        out_shape=[jax.ShapeDtypeStruct((nb, 512), BF16), jax.ShapeDtypeStruct(state.shape, F32)],
        scratch_shapes=[pltpu.VMEM((nb, RET_DK), F32)],
        compiler_params=_params(("parallel",)),
        name="ret_sample",
    )(rq, rk, qt, kt, rv, rg, cos_b, sin_b, cos_t, sin_t, state)


def _bucket_np(d):
    d = np.maximum(d, 0)
    lb = REL_MAX_EXACT + (np.log(np.maximum(d, 1).astype(np.float32) / np.float32(REL_MAX_EXACT))
                          / np.float32(math.log(REL_MAX_DIST / REL_MAX_EXACT))
                          * np.float32(REL_BUCKETS - REL_MAX_EXACT)).astype(np.int32)
    return np.where(d < REL_MAX_EXACT, d, np.minimum(lb, REL_BUCKETS - 1)).astype(np.int32)


def _bias_kernel(rb_ref, bidx_ref, bidx_s_ref, out_ref, out_s_ref):
    for m in range(2):
        bi = bidx_ref[m]
        for h in range(DSA_HEADS):
            acc = jnp.zeros((LANES, LANES), F32)
            for b in range(REL_BUCKETS):
                acc = jnp.where(bi == b, rb_ref[b, h], acc)
            out_ref[h, m] = acc
    for m in range(3):
        bi = bidx_s_ref[m:m + 1, :]
        for h in range(DSA_HEADS):
            acc = jnp.zeros((1, LANES), F32)
            for b in range(REL_BUCKETS):
                acc = jnp.where(bi == b, rb_ref[b, h], acc)
            out_s_ref[m, h:h + 1, :] = acc


def _bias_tables(rel_bias):
    r = np.arange(LANES)
    d = r[:, None] - r[None, :]
    bidx = jnp.asarray(np.stack([_bucket_np(d), _bucket_np(d + LANES)]), I32)
    assert int(_bucket_np(np.array([LANES]))[0]) == REL_BUCKETS - 1
    bidx_s = jnp.asarray(np.stack([np.full(LANES, REL_BUCKETS - 1), _bucket_np(LANES - r), np.zeros(LANES)]), I32)
    return pl.pallas_call(
        _bias_kernel,
        in_specs=[pl.BlockSpec(memory_space=pltpu.SMEM), pl.BlockSpec(memory_space=pltpu.VMEM),
                  pl.BlockSpec(memory_space=pltpu.VMEM)],
        out_specs=[pl.BlockSpec(memory_space=pltpu.VMEM), pl.BlockSpec(memory_space=pltpu.VMEM)],
        out_shape=[jax.ShapeDtypeStruct((DSA_HEADS, 2, LANES, LANES), F32),
                   jax.ShapeDtypeStruct((3, DSA_HEADS, LANES), F32)],
        name="bias_tables",
    )(rel_bias, bidx, bidx_s)


def _order_key(x):
    bits = pltpu.bitcast(jnp.where(x == 0.0, 0.0, x), I32)
    return jnp.where(bits < 0, bits ^ jnp.int32(0x7FFFFFFF), bits)


def _count(mask):
    return jnp.sum(jnp.where(mask, 1.0, 0.0), axis=1, keepdims=True)


def _select_mask(key_ref, j0_ref, n_sel, col_bits, row_ok=None):
    rows, width = key_ref.shape

    def search(it, res):
        cand = res | lax.shift_left(jnp.int32(1), 31 - it)
        cnt = _count(key_ref[...] >= (cand ^ INT_MIN))
        return jnp.where(cnt >= n_sel, cand, res)

    thr = lax.fori_loop(0, 32, search, jnp.zeros((rows, 1), I32)) ^ INT_MIN
    key = key_ref[...]
    need = n_sel - _count(key > thr)
    n_ge = jnp.where(thr == INT_MIN, 0.0, _count(key >= thr))
    if row_ok is not None:
        n_ge = jnp.where(row_ok, n_ge, 0.0)
    j0_ref[...] = jnp.full((rows, 1), width, I32)

    @pl.when(jnp.max(n_ge) > n_sel)
    def _():
        def tie_search(it, res):
            cand = res | lax.shift_left(jnp.int32(1), col_bits - 1 - it)
            col = lax.broadcasted_iota(I32, (rows, width), 1)
            before = jnp.sum(jnp.where(key_ref[...] == thr, jnp.where(col < cand, 1.0, 0.0), 0.0),
                             axis=1, keepdims=True)
            return jnp.where(before < need, cand, res)

        j0_ref[...] = lax.fori_loop(0, col_bits, tie_search, jnp.zeros((rows, 1), I32))

    col = lax.broadcasted_iota(I32, (rows, width), 1)
    ninf = jnp.float32(-jnp.inf)
    tie_ok = jnp.where(key == thr, jnp.where(col <= j0_ref[...], 0.0, ninf), ninf)
    return jnp.where(key > thr, 0.0, tie_ok)


def _dsa_prompt_tile(i, tkw, bfar_ref, dq_ref, iq_ref, tq_ref, bt_ref, o_ref,
                     kpad, vpad, kipad, key_scr, am_scr, lg_scr, j0_scr, *, t_len, n_sel, col_bits):
    key_v = key_scr.at[:, 0:tkw]
    am_v = am_scr.at[:, 0:tkw]
    lg_v = lg_scr.at[:, 0:tkw]

    rowpos = i * LANES + lax.broadcasted_iota(I32, (LANES, 1), 0)
    row_ok = rowpos < t_len

    qi = jnp.where(row_ok, iq_ref[...], 0.0).astype(BF16)
    w = jnp.where(row_ok, tq_ref[:, IDX_DIM:IDX_DIM + DSA_HEADS], 0.0)
    ki = kipad[0:tkw, 0:IDX_DIM]
    sc = None
    for h in range(DSA_HEADS):
        d = lax.dot_general(qi[:, h * IDX_DIM:(h + 1) * IDX_DIM], ki, NT, preferred_element_type=F32)
        t = w[:, h:h + 1] * jnp.maximum(d, 0.0)
        sc = t if sc is None else sc + t

    col = lax.broadcasted_iota(I32, (LANES, tkw), 1)
    vis = col <= rowpos
    key_v[...] = jnp.where(vis, _order_key(sc), INT_MIN)
    am_v[...] = jnp.where(vis, _select_mask(key_v, j0_scr, n_sel, col_bits, row_ok), -jnp.inf)

    q = jnp.where(row_ok, dq_ref[...], 0.0) * (DSA_DH ** -0.5)
    c0 = pl.multiple_of(i * LANES, LANES)
    c1 = pl.multiple_of(jnp.maximum(i - 1, 0) * LANES, LANES)
    for h in range(DSA_HEADS):
        hs = slice(h * DSA_DH, (h + 1) * DSA_DH)
        qh = q[:, hs].astype(BF16)
        lg_v[...] = (lax.dot_general(qh, kpad[0:tkw, hs], NT, preferred_element_type=F32)
                     + bfar_ref[h] + am_v[...])
        for m, cm in ((1, c1), (0, c0)):
            blk = lax.dot_general(qh, kpad[pl.ds(cm, LANES), hs], NT, preferred_element_type=F32)
            lg_scr[:, pl.ds(cm, LANES)] = blk + bt_ref[h, m] + am_scr[:, pl.ds(cm, LANES)]
        lg = lg_v[...]
        mx = jnp.max(lg, axis=1, keepdims=True)
        p = jnp.exp(lg - mx)
        den = jnp.sum(p, axis=1, keepdims=True)
        oh = jnp.dot(p.astype(BF16), vpad[0:tkw, hs], preferred_element_type=F32) / den
        o_ref[:, hs] = oh.astype(BF16)


def _dsa_prompt_kernel(bfar_ref, dq_ref, iq_ref, tq_ref, dk_ref, dv_ref, tk_ref, bt_ref, o_ref,
                       kpad, vpad, kipad, key_scr, am_scr, lg_scr, j0_scr, *, t_len, tk_pad, widths, n_sel, col_bits):
    i = pl.program_id(1)

    @pl.when(i == 0)
    def _():
        kpad[0:t_len] = dk_ref[...].astype(BF16)
        kpad[t_len:tk_pad] = jnp.zeros((tk_pad - t_len, DSA_W), BF16)
        vpad[0:t_len] = dv_ref[...].astype(BF16)
        vpad[t_len:tk_pad] = jnp.zeros((tk_pad - t_len, DSA_W), BF16)
        kipad[0:t_len] = tk_ref[...].astype(BF16)
        kipad[t_len:tk_pad] = jnp.zeros((tk_pad - t_len, PROJ_TAIL), BF16)

    lo = 0
    for nblk in widths:
        @pl.when(jnp.logical_and(i >= lo, i < nblk))
        def _(nblk=nblk):
            _dsa_prompt_tile(i, nblk * LANES, bfar_ref, dq_ref, iq_ref, tq_ref, bt_ref, o_ref,
                             kpad, vpad, kipad, key_scr, am_scr, lg_scr, j0_scr,
                             t_len=t_len, n_sel=n_sel, col_bits=col_bits)
        lo = nblk


def _dsa_prompt(dq, dk, dv, iq, tail, bias_tiles, bias_far):
    b, t, _ = dq.shape
    n_sel = min(TOPK_MAX, t // 4)
    nq = pl.cdiv(t, LANES)
    tk_pad = nq * LANES
    col_bits = max(1, int(math.ceil(math.log2(tk_pad + 1))))
    widths = sorted({int(math.ceil(nq * j / 4)) for j in range(1, 5)})
    qblk = lambda wd: pl.BlockSpec((None, LANES, wd), lambda bi, qi: (bi, qi, 0))
    kblk = lambda wd: pl.BlockSpec((None, t, wd), lambda bi, qi: (bi, 0, 0))
    return pl.pallas_call(
        functools.partial(_dsa_prompt_kernel, t_len=t, tk_pad=tk_pad, widths=widths, n_sel=n_sel, col_bits=col_bits),
        grid=(b, nq),
        in_specs=[pl.BlockSpec(memory_space=pltpu.SMEM), qblk(DSA_W), qblk(DSA_W), qblk(PROJ_TAIL),
                  kblk(DSA_W), kblk(DSA_W), kblk(PROJ_TAIL),
                  pl.BlockSpec((DSA_HEADS, 2, LANES, LANES), lambda bi, qi: (0, 0, 0, 0))],
        out_specs=qblk(DSA_W),
        out_shape=jax.ShapeDtypeStruct((b, t, DSA_W), BF16),
        scratch_shapes=[pltpu.VMEM((tk_pad, DSA_W), BF16), pltpu.VMEM((tk_pad, DSA_W), BF16),
                        pltpu.VMEM((tk_pad, PROJ_TAIL), BF16), pltpu.VMEM((LANES, tk_pad), I32),
                        pltpu.VMEM((LANES, tk_pad), F32), pltpu.VMEM((LANES, tk_pad), F32),
                        pltpu.VMEM((LANES, 1), I32)],
        compiler_params=_params(("parallel", "arbitrary")),
        name="dsa_prompt",
    )(bias_far, dq, iq, tail, dk, dv, tail, bias_tiles)


def _idx_scores_kernel(pt_ref, qi_ref, w_ref, kin_ref, kidx_hbm, out_ref, buf, sem, *, n_pages, page, past):
    b = pl.program_id(0)
    slot = b % 2

    def page_copy(sl, p, src_page):
        dst = buf.at[sl, :, pl.ds(pl.multiple_of(p * page, page), page)]
        return pltpu.make_async_copy(kidx_hbm.at[src_page], dst, sem.at[sl])

    def start_all(bb, sl):
        def start(p, c):
            page_copy(sl, p, pt_ref[bb, p]).start()
            return c

        lax.fori_loop(0, n_pages, start, 0)

    @pl.when(b == 0)
    def _():
        start_all(b, slot)

    @pl.when(b + 1 < pl.num_programs(0))
    def _():
        start_all(b + 1, 1 - slot)

    def wait(p, c):
        page_copy(slot, p, 0).wait()
        return c

    lax.fori_loop(0, n_pages, wait, 0)
    qi = qi_ref[...].astype(BF16)
    w = w_ref[...]
    d = jnp.dot(qi, buf[slot].astype(BF16), preferred_element_type=F32)
    out_ref[:, 0:past] = jnp.sum(w * jnp.maximum(d, 0.0), axis=0, keepdims=True)
    kn = kin_ref[...].astype(BF16).astype(F32)
    dn = jnp.sum(qi.astype(F32) * kn, axis=1, keepdims=True)
    sn = jnp.sum(w * jnp.maximum(dn, 0.0), axis=0, keepdims=True)
    lane = lax.broadcasted_iota(I32, (1, page), 1)
    out_ref[:, past:past + page] = jnp.where(lane == 0, sn, -jnp.inf)


def _idx_scores(page_table, qi, w, ki_new, kidx_t):
    nb, n_pages = page_table.shape
    page = kidx_t.shape[2]
    past = n_pages * page
    width = past + page
    grid_spec = pltpu.PrefetchScalarGridSpec(
        num_scalar_prefetch=1,
        grid=(nb,),
        in_specs=[pl.BlockSpec((None, DSA_HEADS, IDX_DIM), lambda b, pt: (b, 0, 0)),
                  pl.BlockSpec((None, DSA_HEADS, 1), lambda b, pt: (b, 0, 0)),
                  pl.BlockSpec((None, 1, IDX_DIM), lambda b, pt: (b, 0, 0)),
                  pl.BlockSpec(memory_space=pl.ANY)],
        out_specs=pl.BlockSpec((None, 1, width), lambda b, pt: (b, 0, 0)),
        scratch_shapes=[pltpu.VMEM((2, IDX_DIM, past), F32), pltpu.SemaphoreType.DMA((2,))],
    )
    return pl.pallas_call(
        functools.partial(_idx_scores_kernel, n_pages=n_pages, page=page, past=past),
        grid_spec=grid_spec,
        out_shape=jax.ShapeDtypeStruct((nb, 1, width), F32),
        compiler_params=_params(("arbitrary",)),
        name="idx_scores",
    )(page_table, qi, w, ki_new, kidx_t)


def _sel_mask_kernel(s_ref, am_ref, key_scr, j0_scr, *, n_sel, col_bits):
    key_scr[...] = _order_key(s_ref[...])
    am_ref[...] = _select_mask(key_scr, j0_scr, n_sel, col_bits)


def _sel_mask(scores, n_sel):
    nb, width = scores.shape
    col_bits = max(1, int(math.ceil(math.log2(width + 1))))
    return pl.pallas_call(
        functools.partial(_sel_mask_kernel, n_sel=n_sel, col_bits=col_bits),
        in_specs=[pl.BlockSpec(memory_space=pltpu.VMEM)],
        out_specs=pl.BlockSpec(memory_space=pltpu.VMEM),
        out_shape=jax.ShapeDtypeStruct((nb, width), F32),
        scratch_shapes=[pltpu.VMEM((nb, width), I32), pltpu.VMEM((nb, 1), I32)],
        compiler_params=pltpu.CompilerParams(vmem_limit_bytes=VMEM_LIMIT),
        name="sel_mask",
    )(scores)


PAGES_PER_STEP = 16


def _hrows(h):
    return slice(h * DSA_DH, (h + 1) * DSA_DH)


def _eye_dh():
    return lax.broadcasted_iota(I32, (DSA_DH, DSA_DH), 0) == lax.broadcasted_iota(I32, (DSA_DH, DSA_DH), 1)


def _sample_begin(q_ref, qb_scr, m_scr, l_scr, acc_scr):
    q = q_ref[...] * (DSA_DH ** -0.5)
    for h in range(DSA_HEADS):
        qrow = jnp.broadcast_to(q[:, _hrows(h)], (DSA_DH, DSA_DH))
        qcol = jnp.sum(jnp.where(_eye_dh(), qrow, 0.0), axis=1, keepdims=True)
        qb_scr[_hrows(h), :] = jnp.broadcast_to(qcol, (DSA_DH, LANES))
    m_scr[...] = jnp.full(m_scr.shape, -1e30, F32)
    l_scr[...] = jnp.zeros_like(l_scr)
    acc_scr[...] = jnp.zeros_like(acc_scr)


def _sample_chunk(kchunk, vchunk, am_row, bs_ref, last_chunk, qb_scr, m_scr, l_scr, acc_scr):
    pps = kchunk.shape[0]
    qb = qb_scr[...]
    lgs = []
    for i in range(pps):
        prod = kchunk[i].reshape(DSA_W, LANES) * qb
        lg_i = jnp.concatenate([jnp.sum(prod[_hrows(h)], axis=0, keepdims=True) for h in range(DSA_HEADS)], axis=0)
        lgs.append(lg_i + (bs_ref[1] if last_chunk and i == pps - 1 else bs_ref[0]))
    lg = jnp.concatenate(lgs, axis=1) + am_row
    m_old = m_scr[:, 0:1]
    m_new = jnp.maximum(m_old, jnp.max(lg, axis=1, keepdims=True))
    alpha = jnp.exp(m_old - m_new)
    p = jnp.exp(lg - m_new)
    l_scr[...] = jnp.broadcast_to(l_scr[:, 0:1] * alpha + jnp.sum(p, axis=1, keepdims=True), l_scr.shape)
    m_scr[...] = jnp.broadcast_to(m_new, m_scr.shape)
    for h in range(DSA_HEADS):
        a = acc_scr[_hrows(h), :] * alpha[h:h + 1, :]
        for i in range(pps):
            a = a + vchunk[i, h] * p[h:h + 1, i * LANES:(i + 1) * LANES]
        acc_scr[_hrows(h), :] = a


def _sample_finish(q_ref, kn_ref, vn_ref, am_new, bs_ref, o_ref, m_scr, l_scr, acc_scr):
    q = q_ref[...] * (DSA_DH ** -0.5)
    kn = kn_ref[...]
    vn = vn_ref[...]
    outs = []
    for h in range(DSA_HEADS):
        lgn = (jnp.sum(q[:, _hrows(h)] * kn[:, _hrows(h)], axis=1, keepdims=True)
               + bs_ref[2][h:h + 1, 0:1] + am_new)
        mo = m_scr[h:h + 1, 0:1]
        mn = jnp.maximum(mo, lgn)
        al = jnp.exp(mo - mn)
        pn = jnp.exp(lgn - mn)
        den = l_scr[h:h + 1, 0:1] * al + pn
        ocol = jnp.sum(acc_scr[_hrows(h), :], axis=1, keepdims=True)
        orow = jnp.sum(jnp.where(_eye_dh(), jnp.broadcast_to(ocol, (DSA_DH, DSA_DH)), 0.0), axis=0, keepdims=True)
        outs.append((orow * al + pn * vn[:, _hrows(h)]) / den)
    o_ref[...] = jnp.concatenate(outs, axis=1).astype(BF16)


def _route_sample_kernel(pt_sm, st_ref, q_ref, kn_ref, vn_ref, am_ref, bs_ref, ck_hbm, cv_hbm,
                         g_ref, e_ref, o_ref, kbuf, vbuf, sem, qb_scr, m_scr, l_scr, acc_scr, *, nb, n_chunks):
    pps = PAGES_PER_STEP
    t = pl.program_id(0)
    n_steps = pl.num_programs(0)
    heads_per_chunk = PEER_HEADS // n_chunks
    past = n_chunks * pps * LANES

    def chunk_copies(step, c, lookup):
        cps = []
        for i in range(pps):
            pg = pt_sm[jnp.minimum(step, nb - 1), c * pps + i] if lookup else 0
            cps.append(pltpu.make_async_copy(ck_hbm.at[pg], kbuf.at[c % 2, i], sem.at[0, c % 2]))
            cps.append(pltpu.make_async_copy(cv_hbm.at[pg], vbuf.at[c % 2, i], sem.at[1, c % 2]))
        return cps

    @pl.when(t == 0)
    def _():
        for cp in chunk_copies(t, 0, True):
            cp.start()

    _sample_begin(q_ref, qb_scr, m_scr, l_scr, acc_scr)
    for c in range(n_chunks):
        if c + 1 < n_chunks:
            for cp in chunk_copies(t, c + 1, True):
                cp.start()
        else:
            @pl.when(t + 1 < n_steps)
            def _():
                for cp in chunk_copies(t + 1, 0, True):
                    cp.start()
        for cp in chunk_copies(t, c, False):
            cp.wait()
        _sample_chunk(kbuf.at[c % 2], vbuf.at[c % 2], am_ref[:, c * pps * LANES:(c + 1) * pps * LANES], bs_ref,
                      c == n_chunks - 1, qb_scr, m_scr, l_scr, acc_scr)
        for hh in range(c * heads_per_chunk, (c + 1) * heads_per_chunk):
            _route_head(st_ref, g_ref, e_ref, hh)
    _sample_finish(q_ref, kn_ref, vn_ref, am_ref[:, past:past + 1], bs_ref, o_ref, m_scr, l_scr, acc_scr)


def _route_sample(st, page_table, dq, dk_new, dv_new, am, bias_rows, ck_t, cv_t):
    n = st.shape[2]
    n_tiles = n // LANES
    nb, n_pages = page_table.shape
    page = ck_t.shape[3]
    pps = PAGES_PER_STEP
    n_chunks = n_pages // pps
    assert n_chunks % 2 == 0 and PEER_HEADS % n_chunks == 0
    n_steps = max(n_tiles, nb)
    tile = lambda t, pt: jnp.minimum(t, n_tiles - 1)
    row = lambda wd: pl.BlockSpec((None, 1, wd), lambda t, pt: (jnp.minimum(t, nb - 1), 0, 0))
    blk = pl.BlockSpec((HK, LANES), lambda t, pt: (0, tile(t, pt)))
    grid_spec = pltpu.PrefetchScalarGridSpec(
        num_scalar_prefetch=1,
        grid=(n_steps,),
        in_specs=[pl.BlockSpec((2 * PEER_HEADS, PEER_NKEYS, LANES), lambda t, pt: (0, 0, tile(t, pt))),
                  row(DSA_W), row(DSA_W), row(DSA_W), row(am.shape[-1]),
                  pl.BlockSpec((3, DSA_HEADS, LANES), lambda t, pt: (0, 0, 0)),
                  pl.BlockSpec(memory_space=pl.ANY), pl.BlockSpec(memory_space=pl.ANY)],
        out_specs=[blk, blk, row(DSA_W)],
        scratch_shapes=[pltpu.VMEM((2, pps, DSA_HEADS, DSA_DH, page), F32),
                        pltpu.VMEM((2, pps, DSA_HEADS, DSA_DH, page), F32),
                        pltpu.SemaphoreType.DMA((2, 2)),
                        pltpu.VMEM((DSA_W, LANES), F32), pltpu.VMEM((DSA_HEADS, LANES), F32),
                        pltpu.VMEM((DSA_HEADS, LANES), F32), pltpu.VMEM((DSA_W, LANES), F32)],
    )
    r3 = lambda a: a.reshape(nb, 1, a.shape[-1])
    return pl.pallas_call(
        functools.partial(_route_sample_kernel, nb=nb, n_chunks=n_chunks),
        grid_spec=grid_spec,
        out_shape=[jax.ShapeDtypeStruct((HK, n), F32), jax.ShapeDtypeStruct((HK, n), I32),
                   jax.ShapeDtypeStruct((nb, 1, DSA_W), BF16)],
        compiler_params=_params(("arbitrary",)),
        name="route_sample",
    )(page_table, st, r3(dq), r3(dk_new), r3(dv_new), r3(am), bias_rows, ck_t, cv_t)


def _tail1_kernel(ry_ref, do_ref, h_ref, wo_ref, g1_ref, b1_ref, wq_ref, sk_ref, h1_ref, h1b_ref, st_ref):
    half = wo_ref.shape[0] // 2
    mix = (jnp.dot(ry_ref[...], wo_ref[0:half], preferred_element_type=F32)
           + jnp.dot(do_ref[...], wo_ref[half:], preferred_element_type=F32))
    h1 = _ln(DN_ALPHA * h_ref[...] + mix) * g1_ref[...] + b1_ref[...]
    h1_ref[...] = h1
    h1b = h1.astype(BF16)
    h1b_ref[...] = h1b
    q = jnp.dot(h1b, wq_ref[...], preferred_element_type=F32)
    for hh in range(PEER_HEADS):
        qh = _ln(q[:, hh * PEER_DKEY:(hh + 1) * PEER_DKEY]).astype(BF16)
        for s in range(2):
            qs = qh[:, s * (PEER_DKEY // 2):(s + 1) * (PEER_DKEY // 2)]
            st_ref[hh * 2 + s] = lax.dot_general(sk_ref[hh, s], qs, NT, preferred_element_type=F32)


def _tail1(ret_y, dsa_o, h, wo, g1, b1, wq, sk, tm):
    n = h.shape[0]
    row = lambda wd: pl.BlockSpec((tm, wd), lambda i: (i, 0))
    fixed = lambda shp: pl.BlockSpec(shp, lambda i: (0,) * len(shp))
    return pl.pallas_call(
        _tail1_kernel,
        grid=(n // tm,),
        in_specs=[row(512), row(512), row(D_MODEL), fixed(wo.shape), fixed((1, D_MODEL)), fixed((1, D_MODEL)),
                  fixed(wq.shape), fixed(sk.shape)],
        out_specs=[row(D_MODEL), row(D_MODEL),
                   pl.BlockSpec((2 * PEER_HEADS, PEER_NKEYS, tm), lambda i: (0, 0, i))],
        out_shape=[jax.ShapeDtypeStruct((n, D_MODEL), F32), jax.ShapeDtypeStruct((n, D_MODEL), BF16),
                   jax.ShapeDtypeStruct((2 * PEER_HEADS, PEER_NKEYS, n), F32)],
        compiler_params=_params(("parallel",)),
        name="tail1",
    )(ret_y, dsa_o, h, wo, g1, b1, wq, sk)


def _top16(s):
    kio = lax.broadcasted_iota(I32, s.shape, 0).astype(F32)
    vals, idxs = [], []
    for _ in range(PEER_TOPK):
        m = jnp.max(s, axis=0, keepdims=True)
        first = jnp.min(jnp.where(s == m, kio, float(s.shape[0])), axis=0, keepdims=True)
        s = jnp.where(kio == first, -jnp.inf, s)
        vals.append(m)
        idxs.append(first)
    return jnp.concatenate(vals, 0), jnp.concatenate(idxs, 0)


def _route_head(st_ref, g_ref, e_ref, hh):
    r8 = lax.broadcasted_iota(I32, (8, LANES), 0).astype(F32)
    v1, i1 = _top16(st_ref[2 * hh])
    v2, i2 = _top16(st_ref[2 * hh + 1])
    vals = [v1[0:8] + v2[0:1], v1[8:16] + v2[0:1]]
    flat = [r8 * 16.0, (r8 + 8.0) * 16.0]
    exp_id = [i1[0:8] * PEER_NKEYS + i2[0:1], i1[8:16] * PEER_NKEYS + i2[0:1]]
    for j in range(1, 8):
        vals.append(v1[0:8] + v2[j:j + 1])
        flat.append(r8 * 16.0 + float(j))
        exp_id.append(i1[0:8] * PEER_NKEYS + i2[j:j + 1])
    vals.append(v1[0:1] + v2[8:16])
    flat.append(r8 + 8.0)
    exp_id.append(i1[0:1] * PEER_NKEYS + i2[8:16])
    cand = jnp.concatenate(vals, 0)
    pos = jnp.concatenate(flat, 0)
    eid = jnp.concatenate(exp_id, 0)
    tops, es = [], []
    for _ in range(PEER_TOPK):
        m = jnp.max(cand, axis=0, keepdims=True)
        pm = jnp.min(jnp.where(cand == m, pos, 1e9), axis=0, keepdims=True)
        hit = pos == pm
        es.append(jnp.max(jnp.where(hit, eid, -1.0), axis=0, keepdims=True))
        cand = jnp.where(hit, -jnp.inf, cand)
        tops.append(m)
    top = jnp.concatenate(tops, 0)
    ex = jnp.exp(top - top[0:1])
    off = hh * PEER_TOPK if isinstance(hh, int) else pl.multiple_of(hh * PEER_TOPK, PEER_TOPK)
    g_ref[pl.ds(off, PEER_TOPK), :] = ex / jnp.sum(ex, axis=0, keepdims=True)
    e_ref[pl.ds(off, PEER_TOPK), :] = jnp.concatenate(es, 0).astype(I32)


def _route_kernel(st_ref, g_ref, e_ref):
    def head(hh, c):
        _route_head(st_ref, g_ref, e_ref, hh)
        return c

    lax.fori_loop(0, PEER_HEADS, head, 0, unroll=2)


def _route(st):
    n = st.shape[2]
    blk = pl.BlockSpec((HK, LANES), lambda i: (0, i))
    return pl.pallas_call(
        _route_kernel,
        grid=(n // LANES,),
        in_specs=[pl.BlockSpec((2 * PEER_HEADS, PEER_NKEYS, LANES), lambda i: (0, 0, i))],
        out_specs=[blk, blk],
        out_shape=[jax.ShapeDtypeStruct((HK, n), F32), jax.ShapeDtypeStruct((HK, n), I32)],
        compiler_params=_params(("parallel",)),
        name="route",
    )(st)


EXPERT_BLOCK = 1024
EXPERT_SUB = 256


def _peer_act_kernel(x_ref, u_ref, e_ref, g_ref, w_ref, acc_scr):
    j = pl.program_id(1)

    @pl.when(j == 0)
    def _():
        acc_scr[...] = jnp.zeros_like(acc_scr)

    x = x_ref[...]
    e = e_ref[...]
    i2 = e & (PEER_NKEYS - 1)
    i1 = e >> 7
    acc = acc_scr[...]
    for c in range(EXPERT_BLOCK // EXPERT_SUB):
        hmat = lax.dot_general(x, u_ref[c * EXPERT_SUB:(c + 1) * EXPERT_SUB, :], NT,
                               preferred_element_type=F32)
        for s in range(EXPERT_SUB // PEER_NKEYS):
            got = jnp.take_along_axis(hmat[:, s * PEER_NKEYS:(s + 1) * PEER_NKEYS], i2, axis=1,
                                      mode="promise_in_bounds")
            row1 = (j * EXPERT_BLOCK + c * EXPERT_SUB) // PEER_NKEYS + s
            acc = jnp.where(i1 == row1, got, acc)
    acc_scr[...] = acc

    @pl.when(j == pl.num_programs(1) - 1)
    def _():
        a = acc_scr[...]
        gelu = 0.5 * a * (1.0 + lax.erf(a * (2.0 ** -0.5)))
        w_ref[...] = g_ref[...] * gelu


def _peer_act(xb, u, e, g, tr):
    n = xb.shape[0]
    row = lambda wd: pl.BlockSpec((tr, wd), lambda i, j: (i, 0))
    return pl.pallas_call(
        _peer_act_kernel,
        grid=(n // tr, u.shape[0] // EXPERT_BLOCK),
        in_specs=[row(D_MODEL), pl.BlockSpec((EXPERT_BLOCK, D_MODEL), lambda i, j: (j, 0)), row(HK), row(HK)],
        out_specs=row(HK),
        out_shape=jax.ShapeDtypeStruct((n, HK), F32),
        scratch_shapes=[pltpu.VMEM((tr, HK), F32)],
        compiler_params=_params(("parallel", "arbitrary")),
        name="peer_act",
    )(xb, u, e, g)


GATE_GROUP = 8


def _peer_out_kernel(e_ref, w_ref, v_ref, h1_ref, g2_ref, b2_ref, y_ref, p_scr, acc_scr):
    k = pl.program_id(1)
    tr = e_ref.shape[0]
    rows_per_step = v_ref.shape[0] // PEER_NKEYS
    sub8 = p_scr.shape[1]

    @pl.when(k == 0)
    def _():
        acc_scr[...] = jnp.zeros_like(acc_scr)
        sub = lax.broadcasted_iota(I32, (PEER_NKEYS, HK), 0)

        def body(grp, c):
            r0 = pl.multiple_of(grp * GATE_GROUP, GATE_GROUP)
            ers = e_ref[pl.ds(r0, GATE_GROUP), :]
            wrs = w_ref[pl.ds(r0, GATE_GROUP), :]
            ps = []
            for t in range(GATE_GROUP):
                er = ers[t:t + 1, :]
                wr = wrs[t:t + 1, :]
                o1 = jnp.where(sub == (er >> 7), 1.0, 0.0).astype(BF16)
                o2 = jnp.where(sub == (er & (PEER_NKEYS - 1)), wr, 0.0).astype(BF16)
                ps.append(lax.dot_general(o1, o2, NT, preferred_element_type=F32))
            x = jnp.stack(ps).reshape(GATE_GROUP, PEER_NKEYS // sub8, sub8, PEER_NKEYS)
            p_scr[:, :, pl.ds(r0, GATE_GROUP), :] = pltpu.einshape("tqsl->qstl", x)
            return c

        lax.fori_loop(0, tr // GATE_GROUP, body, 0, unroll=4)

    tot = None
    for j in range(0, rows_per_step, 2):
        q = (k * rows_per_step + j) // sub8
        lhs = jnp.concatenate([p_scr[q, j % sub8], p_scr[q, (j + 1) % sub8]], axis=1).astype(BF16)
        d = jnp.dot(lhs, v_ref[j * PEER_NKEYS:(j + 2) * PEER_NKEYS, :], preferred_element_type=F32)
        tot = d if tot is None else tot + d
    acc_scr[...] += tot

    @pl.when(k == pl.num_programs(1) - 1)
    def _():
        y_ref[...] = _ln(DN_ALPHA * h1_ref[...] + acc_scr[...]) * g2_ref[...] + b2_ref[...]


def _peer_out(e, w, v, h1, g2, b2, tr, kc):
    n = h1.shape[0]
    row = lambda wd: pl.BlockSpec((tr, wd), lambda i, k: (i, 0))
    vec = pl.BlockSpec((1, D_MODEL), lambda i, k: (0, 0))
    return pl.pallas_call(
        _peer_out_kernel,
        grid=(n // tr, v.shape[0] // kc),
        in_specs=[row(HK), row(HK), pl.BlockSpec((kc, D_MODEL), lambda i, k: (k, 0)), row(D_MODEL), vec, vec],
        out_specs=row(D_MODEL),
        out_shape=jax.ShapeDtypeStruct((n, D_MODEL), F32),
        scratch_shapes=[pltpu.VMEM((PEER_NKEYS // GATE_GROUP, GATE_GROUP, tr, PEER_NKEYS), F32),
                        pltpu.VMEM((tr, D_MODEL), F32)],
        compiler_params=_params(("parallel", "arbitrary")),
        name="peer_out",
    )(e, w, v, h1, g2, b2)


def _peer_experts(g_t, e_t, h1, h1b, wts, tr_act, tr_out, kc):
    _, _, _, g2, b2, _, _, u, v = wts
    g = g_t.T
    e = e_t.T
    w = _peer_act(h1b, u, e, g, tr_act)
    return _peer_out(e, w, v, h1, g2, b2, tr_out, kc)


def kernel(x_prompt, x_sample, cache_k, cache_v, cache_kidx, state_ret, page_table, meta_tokens, rel_bias,
           w_in, w_out, ln1_g, ln1_b, ln2_g, ln2_b, peer_wq, peer_subkeys, peer_u, peer_v):
    assert w_in.shape[0] == 1 and x_sample.shape[1] == 1
    nbp, seq, _ = x_prompt.shape
    t = seq + N_META
    nbs = x_sample.shape[0]
    n_pages = page_table.shape[1]
    page = cache_kidx.shape[2]
    past = n_pages * page
    assert page == LANES and n_pages % PAGES_PER_STEP == 0

    hp = jnp.concatenate([jnp.broadcast_to(meta_tokens[None], (nbp, N_META, D_MODEL)), x_prompt], 1)
    hp = hp.reshape(nbp * t, D_MODEL)
    hs = x_sample.reshape(nbs, D_MODEL)

    w = w_in[0]
    wa = w[:, :PROJ_MAIN].astype(BF16)
    wt = jnp.pad(w[:, PROJ_MAIN:], ((0, 0), (0, PROJ_TAIL - (w.shape[1] - PROJ_MAIN)))).astype(BF16)
    wts = (w_out[0].astype(BF16), ln1_g, ln1_b, ln2_g, ln2_b, peer_wq[0].astype(BF16),
           peer_subkeys[0].astype(BF16), peer_u[0].astype(BF16), peer_v[0].astype(BF16))
    bias_tiles, bias_rows = _bias_tables(rel_bias)

    rq, rk, rv, rg, dq, dk, dv, iq, tail = _inproj(hp, wa, wt, 384)
    b3 = lambda a: a.reshape(nbp, t, a.shape[-1])
    cosf, sins = _rot_tables(jnp.arange(t, dtype=I32))
    ret_y, ret_s = _ret_prompt(b3(rq), b3(rk), b3(rv), b3(rg), cosf, sins)
    dsa_o = _dsa_prompt(b3(dq), b3(dk), b3(dv), b3(iq), b3(tail), bias_tiles, rel_bias[REL_BUCKETS - 1])
    wo, g1, b1, _, _, wq, sk, _, _ = wts
    h1p, h1bp, stp = _tail1(ret_y.reshape(nbp * t, 512), dsa_o.reshape(nbp * t, DSA_W), hp, wo, g1, b1, wq, sk, 384)

    kidx_t = jnp.transpose(cache_kidx[0], (0, 2, 1))
    ck_t = jnp.transpose(cache_k[0], (0, 2, 3, 1))
    cv_t = jnp.transpose(cache_v[0], (0, 2, 3, 1))
    srq, srk, srv, srg, sdq, sdk, sdv, siq, stail = _inproj(hs, wa, wt, nbs)
    sret_y, sret_s = _ret_sample(srq, srk, srv, srg, state_ret[0], past)
    scores = _idx_scores(page_table, siq.reshape(nbs, DSA_HEADS, IDX_DIM),
                         stail[:, IDX_DIM:IDX_DIM + DSA_HEADS].reshape(nbs, DSA_HEADS, 1),
                         stail[:, :IDX_DIM].reshape(nbs, 1, IDX_DIM), kidx_t)
    n_sel = min(TOPK_MAX, (past + 1) // 4)
    am = _sel_mask(scores.reshape(nbs, past + page), n_sel)
    g_tp, e_tp, sdsa_o = _route_sample(stp, page_table, sdq, sdk, sdv, am, bias_rows, ck_t, cv_t)
    yp = _peer_experts(g_tp, e_tp, h1p, h1bp, wts, 1376, 384, 2048)
    y_prompt = yp.reshape(nbp, t, D_MODEL)[:, N_META:]
    h1s, h1bs, sts = _tail1(sret_y, sdsa_o.reshape(nbs, DSA_W), hs, wo, g1, b1, wq, sk, nbs)
    g_ts, e_ts = _route(sts)
    ys = _peer_experts(g_ts, e_ts, h1s, h1bs, wts, nbs, nbs, 2048)

    kv_p = lambda a: a.reshape(1, nbp, t, DSA_HEADS, DSA_DH)
    kv_s = lambda a: a.reshape(1, nbs, 1, DSA_HEADS, DSA_DH)
    return (y_prompt, ys.reshape(nbs, 1, D_MODEL), kv_p(dk), kv_p(dv),
            tail[:, :IDX_DIM].reshape(1, nbp, t, IDX_DIM), ret_s[None],
            kv_s(sdk), kv_s(sdv), stail[:, :IDX_DIM].reshape(1, nbs, 1, IDX_DIM), sret_s[None])
```

```python
import functools
import math

import numpy as np
import jax
import jax.numpy as jnp
from jax import lax
from jax.experimental import pallas as pl
from jax.experimental.pallas import tpu as pltpu

F32 = jnp.float32
BF16 = jnp.bfloat16
I32 = jnp.int32

D_MODEL = 1024
N_META = 16
RET_HEADS = 4
RET_DK = 128
DSA_HEADS = 8
DSA_DH = 64
DSA_W = DSA_HEADS * DSA_DH
IDX_DIM = 64
TOPK_MAX = 256
REL_BUCKETS = 32
REL_MAX_EXACT = 16
REL_MAX_DIST = 128
PEER_HEADS = 8
PEER_NKEYS = 128
PEER_DKEY = 256
PEER_TOPK = 16
HK = PEER_HEADS * PEER_TOPK
DN_ALPHA = 2.0 ** 0.25
LN_EPS = 1e-5
PROJ_MAIN = 4096
PROJ_TAIL = 128

LANES = 128
INT_MIN = -2 ** 31
VMEM_LIMIT = 56 * 1024 * 1024

NT = (((1,), (1,)), ((), ()))


def _params(sem, vmem=VMEM_LIMIT):
    return pltpu.CompilerParams(dimension_semantics=sem, vmem_limit_bytes=vmem)


def _ln(x):
    mu = jnp.mean(x, axis=-1, keepdims=True)
    xc = x - mu
    var = jnp.mean(xc * xc, axis=-1, keepdims=True)
    return xc * lax.rsqrt(var + LN_EPS)


def _inproj_kernel(x_ref, wa_ref, wt_ref, *out_refs):
    x = x_ref[...].astype(BF16)
    for j in range(8):
        out_refs[j][...] = jnp.dot(x, wa_ref[:, j * 512:(j + 1) * 512], preferred_element_type=F32)
    out_refs[8][...] = jnp.dot(x, wt_ref[...], preferred_element_type=F32)


def _inproj(x, wa, wt, tm):
    n = x.shape[0]
    row = lambda i: (i, 0)
    fixed = lambda i: (0, 0)
    return pl.pallas_call(
        _inproj_kernel,
        grid=(n // tm,),
        in_specs=[pl.BlockSpec((tm, D_MODEL), row), pl.BlockSpec((D_MODEL, PROJ_MAIN), fixed),
                  pl.BlockSpec((D_MODEL, PROJ_TAIL), fixed)],
        out_specs=[pl.BlockSpec((tm, 512), row)] * 8 + [pl.BlockSpec((tm, PROJ_TAIL), row)],
        out_shape=[jax.ShapeDtypeStruct((n, 512), F32)] * 8 + [jax.ShapeDtypeStruct((n, PROJ_TAIL), F32)],
        compiler_params=_params(("parallel",)),
        name="inproj",
    )(x, wa, wt)


def _ret_gammas():
    return [float(np.exp(np.log(np.float32(1.0 - 2.0 ** (-5.0 - h))))) for h in range(RET_HEADS)]


def _ret_tables(last_rows):
    c = LANES
    lg = np.log(1.0 - 2.0 ** (-5.0 - np.arange(RET_HEADS, dtype=np.float64)))
    i = np.arange(c, dtype=np.float64)
    diff = i[:, None] - i[None, :]
    dmask = np.where(diff[None] >= 0, np.exp(np.maximum(diff[None], 0.0) * lg[:, None, None]), 0.0)
    cdec = np.exp((i[None, :] + 1.0) * lg[:, None])[:, :, None] * np.ones((1, 1, c))
    kfull = np.exp((c - 1.0 - i)[None, :] * lg[:, None])
    klast = np.where(i[None, :] < last_rows, np.exp(np.maximum(last_rows - 1.0 - i, 0.0)[None, :] * lg[:, None]), 0.0)
    kdec = np.stack([kfull, klast])[:, :, :, None] * np.ones((1, 1, 1, c))
    g_full = [float(np.exp(c * l)) for l in lg]
    g_last = [float(np.exp(last_rows * l)) for l in lg]
    return (jnp.asarray(dmask, F32), jnp.asarray(cdec, F32), jnp.asarray(kdec, F32), g_full, g_last)


def _rot_tables(pos):
    half = RET_DK // 2
    inv = 1.0 / (10000.0 ** (jnp.arange(half, dtype=F32) / half))
    ang = pos.astype(F32)[:, None] * inv[None, :]
    cos, sin = jnp.cos(ang), jnp.sin(ang)
    return jnp.concatenate([cos, cos], -1), jnp.concatenate([-sin, sin], -1)


def _ret_prompt_kernel(q_ref, k_ref, v_ref, g_ref, cos_ref, sin_ref, dmask_ref, cdec_ref, kdec_ref,
                       y_ref, s_out_ref, s_scr, *, t_len, g_full, g_last):
    c = pl.program_id(1)
    is_last = c == pl.num_programs(1) - 1

    @pl.when(c == 0)
    def _():
        s_scr[...] = jnp.zeros_like(s_scr)

    row = lax.broadcasted_iota(I32, (LANES, 1), 0) + c * LANES
    valid = row < t_len
    cosf = cos_ref[...]
    sins = sin_ref[...]
    scale = RET_DK ** -0.5
    for h in range(RET_HEADS):
        sl = slice(h * RET_DK, (h + 1) * RET_DK)
        q = q_ref[:, sl]
        k = k_ref[:, sl]
        qr = jnp.where(valid, (q * cosf + pltpu.roll(q, RET_DK // 2, 1) * sins) * scale, 0.0)
        kr = jnp.where(valid, k * cosf + pltpu.roll(k, RET_DK // 2, 1) * sins, 0.0)
        v = jnp.where(valid, v_ref[:, sl], 0.0)
        qb = qr.astype(BF16)
        kb = kr.astype(BF16)
        vb = v.astype(BF16)
        sc = lax.dot_general(qb, kb, NT, preferred_element_type=F32) * dmask_ref[h]
        inner = jnp.dot(sc.astype(BF16), vb, preferred_element_type=F32)
        s_old = s_scr[h]
        cross = jnp.dot(qb, s_old.astype(BF16), preferred_element_type=F32) * cdec_ref[h]
        o = inner + cross
        kd_t = (kr * kdec_ref[h]).T.astype(BF16)
        gdec = jnp.where(is_last, g_last[h], g_full[h])
        s_scr[h] = gdec * s_old + jnp.dot(kd_t, vb, preferred_element_type=F32)
        g = g_ref[:, sl]
        y_ref[:, sl] = (_ln(o) * (g * jax.nn.sigmoid(g))).astype(BF16)

    @pl.when(is_last)
    def _():
        s_out_ref[...] = s_scr[...]


def _ret_prompt(rq, rk, rv, rg, cosf, sins):
    b, t, _ = rq.shape
    nc = pl.cdiv(t, LANES)
    last_rows = t - (nc - 1) * LANES
    dmask, cdec, kdec, g_full, g_last = _ret_tables(last_rows)
    blk = pl.BlockSpec((None, LANES, 512), lambda bi, ci: (bi, ci, 0))
    tab = pl.BlockSpec((LANES, RET_DK), lambda bi, ci: (ci, 0))
    full3 = pl.BlockSpec((RET_HEADS, LANES, LANES), lambda bi, ci: (0, 0, 0))
    return pl.pallas_call(
        functools.partial(_ret_prompt_kernel, t_len=t, g_full=g_full, g_last=g_last),
        grid=(b, nc),
        in_specs=[blk, blk, blk, blk, tab, tab, full3, full3,
                  pl.BlockSpec((None, RET_HEADS, LANES, LANES), lambda bi, ci: (ci // (nc - 1), 0, 0, 0))],
        out_specs=[blk, pl.BlockSpec((None, RET_HEADS, RET_DK, RET_DK), lambda bi, ci: (bi, 0, 0, 0))],
        out_shape=[jax.ShapeDtypeStruct((b, t, 512), BF16),
                   jax.ShapeDtypeStruct((b, RET_HEADS, RET_DK, RET_DK), F32)],
        scratch_shapes=[pltpu.VMEM((RET_HEADS, RET_DK, RET_DK), F32)],
        compiler_params=_params(("parallel", "arbitrary")),
        name="ret_prompt",
    )(rq, rk, rv, rg, cosf, sins, dmask, cdec, kdec)


def _ret_sample_kernel(q_ref, k_ref, qt_ref, kt_ref, v_ref, g_ref, cos_ref, sin_ref, cost_ref, sint_ref,
                       s0_ref, y_ref, s1_ref, o_scr, *, gammas):
    h = pl.program_id(0)
    gamma = jnp.float32(gammas[0])
    for i in range(1, RET_HEADS):
        gamma = jnp.where(h == i, jnp.float32(gammas[i]), gamma)
    scale = RET_DK ** -0.5
    half = RET_DK // 2

    def rot_t(x):
        return x * cost_ref[...] + jnp.concatenate([x[half:], x[:half]], axis=0) * sint_ref[...]

    def rot(x):
        return x * cos_ref[...] + pltpu.roll(x, half, 1) * sin_ref[...]

    def r16(x):
        return x.astype(BF16).astype(F32)

    qt = r16(rot_t(qt_ref[...]) * scale)
    kt = r16(rot_t(kt_ref[...]))
    q = r16(rot(q_ref[...]) * scale)
    k = r16(rot(k_ref[...]))
    vb = r16(v_ref[...])
    qk = r16(jnp.sum(q * k, axis=1, keepdims=True))
    inner = qk * vb
    for b in range(q.shape[0]):
        s_old = s0_ref[b]
        qc = qt[:, b:b + 1]
        kc = kt[:, b:b + 1]
        o_scr[b:b + 1, :] = jnp.sum(qc * r16(s_old), axis=0, keepdims=True) * gamma
        s1_ref[b] = gamma * s_old + kc * vb[b:b + 1, :]
    o = inner + o_scr[...]
    g = g_ref[...]
    y_ref[...] = (_ln(o) * (g * jax.nn.sigmoid(g))).astype(BF16)


def _ret_sample(rq, rk, rv, rg, state, pos):
    nb = rq.shape[0]
    cosf, sins = _rot_tables(jnp.full((1,), pos, I32))
    cos_b = jnp.broadcast_to(cosf, (nb, RET_DK))
    sin_b = jnp.broadcast_to(sins, (nb, RET_DK))
    cos_t = jnp.broadcast_to(cosf.reshape(RET_DK, 1), (RET_DK, nb))
    sin_t = jnp.broadcast_to(sins.reshape(RET_DK, 1), (RET_DK, nb))
    qt = rq.reshape(nb, RET_HEADS, RET_DK).transpose(1, 2, 0)
    kt = rk.reshape(nb, RET_HEADS, RET_DK).transpose(1, 2, 0)
    col = pl.BlockSpec((nb, RET_DK), lambda h: (0, h))
    tr = pl.BlockSpec((None, RET_DK, nb), lambda h: (h, 0, 0))
    full = pl.BlockSpec((nb, RET_DK), lambda h: (0, 0))
    full_t = pl.BlockSpec((RET_DK, nb), lambda h: (0, 0))
    st = pl.BlockSpec((nb, None, RET_DK, RET_DK), lambda h: (0, h, 0, 0))
    return pl.pallas_call(
        functools.partial(_ret_sample_kernel, gammas=_ret_gammas()),
        grid=(RET_HEADS,),
        in_specs=[col, col, tr, tr, col, col, full, full, full_t, full_t, st],
        out_specs=[col, st],
        out_shape=[jax.ShapeDtypeStruct((nb, 512), BF16), jax.ShapeDtypeStruct(state.shape, F32)],
        scratch_shapes=[pltpu.VMEM((nb, RET_DK), F32)],
        compiler_params=_params(("parallel",)),
        name="ret_sample",
    )(rq, rk, qt, kt, rv, rg, cos_b, sin_b, cos_t, sin_t, state)


def _bucket_np(d):
    d = np.maximum(d, 0)
    lb = REL_MAX_EXACT + (np.log(np.maximum(d, 1).astype(np.float32) / np.float32(REL_MAX_EXACT))
                          / np.float32(math.log(REL_MAX_DIST / REL_MAX_EXACT))
                          * np.float32(REL_BUCKETS - REL_MAX_EXACT)).astype(np.int32)
    return np.where(d < REL_MAX_EXACT, d, np.minimum(lb, REL_BUCKETS - 1)).astype(np.int32)


def _bias_kernel(rb_ref, bidx_ref, bidx_s_ref, out_ref, out_s_ref):
    for m in range(2):
        bi = bidx_ref[m]
        for h in range(DSA_HEADS):
            acc = jnp.zeros((LANES, LANES), F32)
            for b in range(REL_BUCKETS):
                acc = jnp.where(bi == b, rb_ref[b, h], acc)
            out_ref[h, m] = acc
    for m in range(3):
        bi = bidx_s_ref[m:m + 1, :]
        for h in range(DSA_HEADS):
            acc = jnp.zeros((1, LANES), F32)
            for b in range(REL_BUCKETS):
                acc = jnp.where(bi == b, rb_ref[b, h], acc)
            out_s_ref[m, h:h + 1, :] = acc


def _bias_tables(rel_bias):
    r = np.arange(LANES)
    d = r[:, None] - r[None, :]
    bidx = jnp.asarray(np.stack([_bucket_np(d), _bucket_np(d + LANES)]), I32)
    assert int(_bucket_np(np.array([LANES]))[0]) == REL_BUCKETS - 1
    bidx_s = jnp.asarray(np.stack([np.full(LANES, REL_BUCKETS - 1), _bucket_np(LANES - r), np.zeros(LANES)]), I32)
    return pl.pallas_call(
        _bias_kernel,
        in_specs=[pl.BlockSpec(memory_space=pltpu.SMEM), pl.BlockSpec(memory_space=pltpu.VMEM),
                  pl.BlockSpec(memory_space=pltpu.VMEM)],
        out_specs=[pl.BlockSpec(memory_space=pltpu.VMEM), pl.BlockSpec(memory_space=pltpu.VMEM)],
        out_shape=[jax.ShapeDtypeStruct((DSA_HEADS, 2, LANES, LANES), F32),
                   jax.ShapeDtypeStruct((3, DSA_HEADS, LANES), F32)],
        name="bias_tables",
    )(rel_bias, bidx, bidx_s)


def _order_key(x):
    bits = pltpu.bitcast(jnp.where(x == 0.0, 0.0, x), I32)
    return jnp.where(bits < 0, bits ^ jnp.int32(0x7FFFFFFF), bits)


def _count(mask):
    return jnp.sum(jnp.where(mask, 1.0, 0.0), axis=1, keepdims=True)


def _select_mask(key_ref, j0_ref, n_sel, col_bits, row_ok=None):
    rows, width = key_ref.shape

    def search(it, res):
        cand = res | lax.shift_left(jnp.int32(1), 31 - it)
        cnt = _count(key_ref[...] >= (cand ^ INT_MIN))
        return jnp.where(cnt >= n_sel, cand, res)

    thr = lax.fori_loop(0, 32, search, jnp.zeros((rows, 1), I32), unroll=4) ^ INT_MIN
    key = key_ref[...]
    need = n_sel - _count(key > thr)
    n_ge = jnp.where(thr == INT_MIN, 0.0, _count(key >= thr))
    if row_ok is not None:
        n_ge = jnp.where(row_ok, n_ge, 0.0)
    j0_ref[...] = jnp.full((rows, 1), width, I32)

    @pl.when(jnp.max(n_ge) > n_sel)
    def _():
        def tie_search(it, res):
            cand = res | lax.shift_left(jnp.int32(1), col_bits - 1 - it)
            col = lax.broadcasted_iota(I32, (rows, width), 1)
            before = jnp.sum(jnp.where(key_ref[...] == thr, jnp.where(col < cand, 1.0, 0.0), 0.0),
                             axis=1, keepdims=True)
            return jnp.where(before < need, cand, res)

        j0_ref[...] = lax.fori_loop(0, col_bits, tie_search, jnp.zeros((rows, 1), I32))

    col = lax.broadcasted_iota(I32, (rows, width), 1)
    ninf = jnp.float32(-jnp.inf)
    tie_ok = jnp.where(key == thr, jnp.where(col <= j0_ref[...], 0.0, ninf), ninf)
    return jnp.where(key > thr, 0.0, tie_ok)


def _dsa_prompt_tile(i, tkw, bfar_ref, dq_ref, iq_ref, tq_ref, bt_ref, o_ref,
                     kpad, vpad, kipad, key_scr, am_scr, lg_scr, j0_scr, *, t_len, n_sel, col_bits):
    key_v = key_scr.at[:, 0:tkw]
    am_v = am_scr.at[:, 0:tkw]
    lg_v = lg_scr.at[:, 0:tkw]

    rowpos = i * LANES + lax.broadcasted_iota(I32, (LANES, 1), 0)
    row_ok = rowpos < t_len

    qi = jnp.where(row_ok, iq_ref[...], 0.0).astype(BF16)
    w = jnp.where(row_ok, tq_ref[:, IDX_DIM:IDX_DIM + DSA_HEADS], 0.0)
    ki = kipad[0:tkw, 0:IDX_DIM]
    sc = None
    for h in range(DSA_HEADS):
        d = lax.dot_general(qi[:, h * IDX_DIM:(h + 1) * IDX_DIM], ki, NT, preferred_element_type=F32)
        t = w[:, h:h + 1] * jnp.maximum(d, 0.0)
        sc = t if sc is None else sc + t

    col = lax.broadcasted_iota(I32, (LANES, tkw), 1)
    vis = col <= rowpos
    key_v[...] = jnp.where(vis, _order_key(sc), INT_MIN)
    am_v[...] = jnp.where(vis, _select_mask(key_v, j0_scr, n_sel, col_bits, row_ok), -jnp.inf)

    q = jnp.where(row_ok, dq_ref[...], 0.0) * (DSA_DH ** -0.5)
    c0 = pl.multiple_of(i * LANES, LANES)
    c1 = pl.multiple_of(jnp.maximum(i - 1, 0) * LANES, LANES)
    for h in range(DSA_HEADS):
        hs = slice(h * DSA_DH, (h + 1) * DSA_DH)
        qh = q[:, hs].astype(BF16)
        lg_v[...] = (lax.dot_general(qh, kpad[0:tkw, hs], NT, preferred_element_type=F32)
                     + bfar_ref[h] + am_v[...])
        for m, cm in ((1, c1), (0, c0)):
            blk = lax.dot_general(qh, kpad[pl.ds(cm, LANES), hs], NT, preferred_element_type=F32)
            lg_scr[:, pl.ds(cm, LANES)] = blk + bt_ref[h, m] + am_scr[:, pl.ds(cm, LANES)]
        lg = lg_v[...]
        mx = jnp.max(lg, axis=1, keepdims=True)
        p = jnp.exp(lg - mx)
        den = jnp.sum(p, axis=1, keepdims=True)
        oh = jnp.dot(p.astype(BF16), vpad[0:tkw, hs], preferred_element_type=F32) / den
        o_ref[:, hs] = oh.astype(BF16)


def _dsa_prompt_kernel(bfar_ref, dq_ref, iq_ref, tq_ref, dk_ref, dv_ref, tk_ref, bt_ref, o_ref,
                       kpad, vpad, kipad, key_scr, am_scr, lg_scr, j0_scr, *, t_len, tk_pad, widths, n_sel, col_bits):
    i = pl.program_id(1)

    @pl.when(i == 0)
    def _():
        kpad[0:t_len] = dk_ref[...].astype(BF16)
        kpad[t_len:tk_pad] = jnp.zeros((tk_pad - t_len, DSA_W), BF16)
        vpad[0:t_len] = dv_ref[...].astype(BF16)
        vpad[t_len:tk_pad] = jnp.zeros((tk_pad - t_len, DSA_W), BF16)
        kipad[0:t_len] = tk_ref[...].astype(BF16)
        kipad[t_len:tk_pad] = jnp.zeros((tk_pad - t_len, PROJ_TAIL), BF16)

    lo = 0
    for nblk in widths:
        @pl.when(jnp.logical_and(i >= lo, i < nblk))
        def _(nblk=nblk):
            _dsa_prompt_tile(i, nblk * LANES, bfar_ref, dq_ref, iq_ref, tq_ref, bt_ref, o_ref,
                             kpad, vpad, kipad, key_scr, am_scr, lg_scr, j0_scr,
                             t_len=t_len, n_sel=n_sel, col_bits=col_bits)
        lo = nblk


def _dsa_prompt(dq, dk, dv, iq, tail, bias_tiles, bias_far):
    b, t, _ = dq.shape
    n_sel = min(TOPK_MAX, t // 4)
    nq = pl.cdiv(t, LANES)
    tk_pad = nq * LANES
    col_bits = max(1, int(math.ceil(math.log2(tk_pad + 1))))
    widths = sorted({int(math.ceil(nq * j / 4)) for j in range(1, 5)})
    qblk = lambda wd: pl.BlockSpec((None, LANES, wd), lambda bi, qi: (bi, qi, 0))
    kblk = lambda wd: pl.BlockSpec((None, t, wd), lambda bi, qi: (bi, 0, 0))
    return pl.pallas_call(
        functools.partial(_dsa_prompt_kernel, t_len=t, tk_pad=tk_pad, widths=widths, n_sel=n_sel, col_bits=col_bits),
        grid=(b, nq),
        in_specs=[pl.BlockSpec(memory_space=pltpu.SMEM), qblk(DSA_W), qblk(DSA_W), qblk(PROJ_TAIL),
                  kblk(DSA_W), kblk(DSA_W), kblk(PROJ_TAIL),
                  pl.BlockSpec((DSA_HEADS, 2, LANES, LANES), lambda bi, qi: (0, 0, 0, 0))],
        out_specs=qblk(DSA_W),
        out_shape=jax.ShapeDtypeStruct((b, t, DSA_W), BF16),
        scratch_shapes=[pltpu.VMEM((tk_pad, DSA_W), BF16), pltpu.VMEM((tk_pad, DSA_W), BF16),
                        pltpu.VMEM((tk_pad, PROJ_TAIL), BF16), pltpu.VMEM((LANES, tk_pad), I32),
                        pltpu.VMEM((LANES, tk_pad), F32), pltpu.VMEM((LANES, tk_pad), F32),
                        pltpu.VMEM((LANES, 1), I32)],
        compiler_params=_params(("parallel", "arbitrary")),
        name="dsa_prompt",
    )(bias_far, dq, iq, tail, dk, dv, tail, bias_tiles)


def _idx_scores_kernel(pt_ref, qi_ref, w_ref, kin_ref, kidx_hbm, out_ref, buf, sem, *, n_pages, page, past):
    b = pl.program_id(0)
    slot = b % 2

    def page_copy(sl, p, src_page):
        dst = buf.at[sl, :, pl.ds(pl.multiple_of(p * page, page), page)]
        return pltpu.make_async_copy(kidx_hbm.at[src_page], dst, sem.at[sl])

    def start_all(bb, sl):
        def start(p, c):
            page_copy(sl, p, pt_ref[bb, p]).start()
            return c

        lax.fori_loop(0, n_pages, start, 0)

    @pl.when(b == 0)
    def _():
        start_all(b, slot)

    @pl.when(b + 1 < pl.num_programs(0))
    def _():
        start_all(b + 1, 1 - slot)

    def wait(p, c):
        page_copy(slot, p, 0).wait()
        return c

    lax.fori_loop(0, n_pages, wait, 0)
    qi = qi_ref[...].astype(BF16)
    w = w_ref[...]
    d = jnp.dot(qi, buf[slot].astype(BF16), preferred_element_type=F32)
    out_ref[:, 0:past] = jnp.sum(w * jnp.maximum(d, 0.0), axis=0, keepdims=True)
    kn = kin_ref[...].astype(BF16).astype(F32)
    dn = jnp.sum(qi.astype(F32) * kn, axis=1, keepdims=True)
    sn = jnp.sum(w * jnp.maximum(dn, 0.0), axis=0, keepdims=True)
    lane = lax.broadcasted_iota(I32, (1, page), 1)
    out_ref[:, past:past + page] = jnp.where(lane == 0, sn, -jnp.inf)


def _idx_scores(page_table, qi, w, ki_new, kidx_t):
    nb, n_pages = page_table.shape
    page = kidx_t.shape[2]
    past = n_pages * page
    width = past + page
    grid_spec = pltpu.PrefetchScalarGridSpec(
        num_scalar_prefetch=1,
        grid=(nb,),
        in_specs=[pl.BlockSpec((None, DSA_HEADS, IDX_DIM), lambda b, pt: (b, 0, 0)),
                  pl.BlockSpec((None, DSA_HEADS, 1), lambda b, pt: (b, 0, 0)),
                  pl.BlockSpec((None, 1, IDX_DIM), lambda b, pt: (b, 0, 0)),
                  pl.BlockSpec(memory_space=pl.ANY)],
        out_specs=pl.BlockSpec((None, 1, width), lambda b, pt: (b, 0, 0)),
        scratch_shapes=[pltpu.VMEM((2, IDX_DIM, past), F32), pltpu.SemaphoreType.DMA((2,))],
    )
    return pl.pallas_call(
        functools.partial(_idx_scores_kernel, n_pages=n_pages, page=page, past=past),
        grid_spec=grid_spec,
        out_shape=jax.ShapeDtypeStruct((nb, 1, width), F32),
        compiler_params=_params(("arbitrary",)),
        name="idx_scores",
    )(page_table, qi, w, ki_new, kidx_t)


def _sel_mask_kernel(s_ref, am_ref, key_scr, j0_scr, *, n_sel, col_bits):
    key_scr[...] = _order_key(s_ref[...])
    am_ref[...] = _select_mask(key_scr, j0_scr, n_sel, col_bits)


def _sel_mask(scores, n_sel):
    nb, width = scores.shape
    col_bits = max(1, int(math.ceil(math.log2(width + 1))))
    return pl.pallas_call(
        functools.partial(_sel_mask_kernel, n_sel=n_sel, col_bits=col_bits),
        in_specs=[pl.BlockSpec(memory_space=pltpu.VMEM)],
        out_specs=pl.BlockSpec(memory_space=pltpu.VMEM),
        out_shape=jax.ShapeDtypeStruct((nb, width), F32),
        scratch_shapes=[pltpu.VMEM((nb, width), I32), pltpu.VMEM((nb, 1), I32)],
        compiler_params=pltpu.CompilerParams(vmem_limit_bytes=VMEM_LIMIT),
        name="sel_mask",
    )(scores)


PAGES_PER_STEP = 16


def _hrows(h):
    return slice(h * DSA_DH, (h + 1) * DSA_DH)


def _eye_dh():
    return lax.broadcasted_iota(I32, (DSA_DH, DSA_DH), 0) == lax.broadcasted_iota(I32, (DSA_DH, DSA_DH), 1)


MXU_ROWS = 16


def _sample_begin(q_ref, qb_scr, m_scr, l_scr, acc_scr):
    q = q_ref[...] * (DSA_DH ** -0.5)
    head_of_col = lax.broadcasted_iota(I32, (MXU_ROWS, DSA_W), 1) // DSA_DH
    row = lax.broadcasted_iota(I32, (MXU_ROWS, DSA_W), 0)
    qb_scr[...] = jnp.where(head_of_col == row, jnp.broadcast_to(q, (MXU_ROWS, DSA_W)), 0.0)
    m_scr[...] = jnp.full(m_scr.shape, -1e30, F32)
    l_scr[...] = jnp.zeros_like(l_scr)
    acc_scr[...] = jnp.zeros_like(acc_scr)


def _sample_chunk(kchunk, vchunk, am_row, bs_ref, last_chunk, qb_scr, m_scr, l_scr, acc_scr):
    pps = kchunk.shape[0]
    qb = qb_scr[...].astype(BF16)
    lgs = []
    for i in range(pps):
        lg_i = jnp.dot(qb, kchunk[i].reshape(DSA_W, LANES).astype(BF16), preferred_element_type=F32)[0:DSA_HEADS]
        lgs.append(lg_i + (bs_ref[1] if last_chunk and i == pps - 1 else bs_ref[0]))
    lg = jnp.concatenate(lgs, axis=1) + am_row
    m_old = m_scr[:, 0:1]
    m_new = jnp.maximum(m_old, jnp.max(lg, axis=1, keepdims=True))
    alpha = jnp.exp(m_old - m_new)
    p = jnp.exp(lg - m_new)
    l_scr[...] = jnp.broadcast_to(l_scr[:, 0:1] * alpha + jnp.sum(p, axis=1, keepdims=True), l_scr.shape)
    m_scr[...] = jnp.broadcast_to(m_new, m_scr.shape)
    vcat = jnp.concatenate([vchunk[i].reshape(DSA_W, LANES).astype(BF16) for i in range(pps)], axis=1)
    p16 = jnp.concatenate([p, jnp.zeros((MXU_ROWS - DSA_HEADS, p.shape[1]), F32)], axis=0).astype(BF16)
    pv = lax.dot_general(vcat, p16, NT, preferred_element_type=F32)
    alpha_rows = jnp.concatenate([jnp.broadcast_to(alpha[h:h + 1, :], (DSA_DH, 1)) for h in range(DSA_HEADS)], axis=0)
    acc_scr[...] = acc_scr[...] * alpha_rows + pv


def _sample_finish(q_ref, kn_ref, vn_ref, am_new, bs_ref, o_ref, m_scr, l_scr, acc_scr):
    q = q_ref[...] * (DSA_DH ** -0.5)
    kn = kn_ref[...]
    vn = vn_ref[...]
    outs = []
    for h in range(DSA_HEADS):
        lgn = (jnp.sum(q[:, _hrows(h)] * kn[:, _hrows(h)], axis=1, keepdims=True)
               + bs_ref[2][h:h + 1, 0:1] + am_new)
        mo = m_scr[h:h + 1, 0:1]
        mn = jnp.maximum(mo, lgn)
        al = jnp.exp(mo - mn)
        pn = jnp.exp(lgn - mn)
        den = l_scr[h:h + 1, 0:1] * al + pn
        ocol = acc_scr[_hrows(h), h:h + 1]
        orow = jnp.sum(jnp.where(_eye_dh(), jnp.broadcast_to(ocol, (DSA_DH, DSA_DH)), 0.0), axis=0, keepdims=True)
        outs.append((orow * al + pn * vn[:, _hrows(h)]) / den)
    o_ref[...] = jnp.concatenate(outs, axis=1).astype(BF16)


def _route_sample_kernel(pt_sm, st_ref, q_ref, kn_ref, vn_ref, am_ref, bs_ref, ck_hbm, cv_hbm,
                         g_ref, e_ref, o_ref, kbuf, vbuf, sem, qb_scr, m_scr, l_scr, acc_scr, *, nb, n_chunks):
    pps = PAGES_PER_STEP
    t = pl.program_id(0)
    n_steps = pl.num_programs(0)
    heads_per_chunk = PEER_HEADS // n_chunks
    past = n_chunks * pps * LANES

    def chunk_copies(step, c, lookup):
        cps = []
        for i in range(pps):
            pg = pt_sm[jnp.minimum(step, nb - 1), c * pps + i] if lookup else 0
            cps.append(pltpu.make_async_copy(ck_hbm.at[pg], kbuf.at[c % 2, i], sem.at[0, c % 2]))
            cps.append(pltpu.make_async_copy(cv_hbm.at[pg], vbuf.at[c % 2, i], sem.at[1, c % 2]))
        return cps

    @pl.when(t == 0)
    def _():
        for cp in chunk_copies(t, 0, True):
            cp.start()

    _sample_begin(q_ref, qb_scr, m_scr, l_scr, acc_scr)
    for c in range(n_chunks):
        if c + 1 < n_chunks:
            for cp in chunk_copies(t, c + 1, True):
                cp.start()
        else:
            @pl.when(t + 1 < n_steps)
            def _():
                for cp in chunk_copies(t + 1, 0, True):
                    cp.start()
        for cp in chunk_copies(t, c, False):
            cp.wait()
        _sample_chunk(kbuf.at[c % 2], vbuf.at[c % 2], am_ref[:, c * pps * LANES:(c + 1) * pps * LANES], bs_ref,
                      c == n_chunks - 1, qb_scr, m_scr, l_scr, acc_scr)
        for hh in range(c * heads_per_chunk, (c + 1) * heads_per_chunk):
            _route_head(st_ref, g_ref, e_ref, hh)
    _sample_finish(q_ref, kn_ref, vn_ref, am_ref[:, past:past + 1], bs_ref, o_ref, m_scr, l_scr, acc_scr)


def _route_sample(st, page_table, dq, dk_new, dv_new, am, bias_rows, ck_t, cv_t):
    n = st.shape[2]
    n_tiles = n // LANES
    nb, n_pages = page_table.shape
    page = ck_t.shape[3]
    pps = PAGES_PER_STEP
    n_chunks = n_pages // pps
    assert n_chunks % 2 == 0 and PEER_HEADS % n_chunks == 0
    n_steps = max(n_tiles, nb)
    tile = lambda t, pt: jnp.minimum(t, n_tiles - 1)
    row = lambda wd: pl.BlockSpec((None, 1, wd), lambda t, pt: (jnp.minimum(t, nb - 1), 0, 0))
    blk = pl.BlockSpec((HK, LANES), lambda t, pt: (0, tile(t, pt)))
    grid_spec = pltpu.PrefetchScalarGridSpec(
        num_scalar_prefetch=1,
        grid=(n_steps,),
        in_specs=[pl.BlockSpec((2 * PEER_HEADS, PEER_NKEYS, LANES), lambda t, pt: (0, 0, tile(t, pt))),
                  row(DSA_W), row(DSA_W), row(DSA_W), row(am.shape[-1]),
                  pl.BlockSpec((3, DSA_HEADS, LANES), lambda t, pt: (0, 0, 0)),
                  pl.BlockSpec(memory_space=pl.ANY), pl.BlockSpec(memory_space=pl.ANY)],
        out_specs=[blk, blk, row(DSA_W)],
        scratch_shapes=[pltpu.VMEM((2, pps, DSA_HEADS, DSA_DH, page), F32),
                        pltpu.VMEM((2, pps, DSA_HEADS, DSA_DH, page), F32),
                        pltpu.SemaphoreType.DMA((2, 2)),
                        pltpu.VMEM((MXU_ROWS, DSA_W), F32), pltpu.VMEM((DSA_HEADS, LANES), F32),
                        pltpu.VMEM((DSA_HEADS, LANES), F32), pltpu.VMEM((DSA_W, MXU_ROWS), F32)],
    )
    r3 = lambda a: a.reshape(nb, 1, a.shape[-1])
    return pl.pallas_call(
        functools.partial(_route_sample_kernel, nb=nb, n_chunks=n_chunks),
        grid_spec=grid_spec,
        out_shape=[jax.ShapeDtypeStruct((HK, n), F32), jax.ShapeDtypeStruct((HK, n), I32),
                   jax.ShapeDtypeStruct((nb, 1, DSA_W), BF16)],
        compiler_params=_params(("arbitrary",)),
        name="route_sample",
    )(page_table, st, r3(dq), r3(dk_new), r3(dv_new), r3(am), bias_rows, ck_t, cv_t)


def _tail1_kernel(ry_ref, do_ref, h_ref, wo_ref, g1_ref, b1_ref, wq_ref, sk_ref, h1_ref, h1b_ref, st_ref):
    half = wo_ref.shape[0] // 2
    mix = (jnp.dot(ry_ref[...], wo_ref[0:half], preferred_element_type=F32)
           + jnp.dot(do_ref[...], wo_ref[half:], preferred_element_type=F32))
    h1 = _ln(DN_ALPHA * h_ref[...] + mix) * g1_ref[...] + b1_ref[...]
    h1_ref[...] = h1
    h1b = h1.astype(BF16)
    h1b_ref[...] = h1b
    q = jnp.dot(h1b, wq_ref[...], preferred_element_type=F32)
    for hh in range(PEER_HEADS):
        qh = _ln(q[:, hh * PEER_DKEY:(hh + 1) * PEER_DKEY]).astype(BF16)
        for s in range(2):
            qs = qh[:, s * (PEER_DKEY // 2):(s + 1) * (PEER_DKEY // 2)]
            st_ref[hh * 2 + s] = lax.dot_general(sk_ref[hh, s], qs, NT, preferred_element_type=F32)


def _tail1(ret_y, dsa_o, h, wo, g1, b1, wq, sk, tm):
    n = h.shape[0]
    row = lambda wd: pl.BlockSpec((tm, wd), lambda i: (i, 0))
    fixed = lambda shp: pl.BlockSpec(shp, lambda i: (0,) * len(shp))
    return pl.pallas_call(
        _tail1_kernel,
        grid=(n // tm,),
        in_specs=[row(512), row(512), row(D_MODEL), fixed(wo.shape), fixed((1, D_MODEL)), fixed((1, D_MODEL)),
                  fixed(wq.shape), fixed(sk.shape)],
        out_specs=[row(D_MODEL), row(D_MODEL),
                   pl.BlockSpec((2 * PEER_HEADS, PEER_NKEYS, tm), lambda i: (0, 0, i))],
        out_shape=[jax.ShapeDtypeStruct((n, D_MODEL), F32), jax.ShapeDtypeStruct((n, D_MODEL), BF16),
                   jax.ShapeDtypeStruct((2 * PEER_HEADS, PEER_NKEYS, n), F32)],
        compiler_params=_params(("parallel",)),
        name="tail1",
    )(ret_y, dsa_o, h, wo, g1, b1, wq, sk)


def _top16(s):
    kio = lax.broadcasted_iota(I32, s.shape, 0).astype(F32)
    vals, idxs = [], []
    for _ in range(PEER_TOPK):
        m = jnp.max(s, axis=0, keepdims=True)
        first = jnp.min(jnp.where(s == m, kio, float(s.shape[0])), axis=0, keepdims=True)
        s = jnp.where(kio == first, -jnp.inf, s)
        vals.append(m)
        idxs.append(first)
    return jnp.concatenate(vals, 0), jnp.concatenate(idxs, 0)


def _route_head(st_ref, g_ref, e_ref, hh):
    r8 = lax.broadcasted_iota(I32, (8, LANES), 0).astype(F32)
    v1, i1 = _top16(st_ref[2 * hh])
    v2, i2 = _top16(st_ref[2 * hh + 1])
    vals = [v1[0:8] + v2[0:1], v1[8:16] + v2[0:1]]
    flat = [r8 * 16.0, (r8 + 8.0) * 16.0]
    exp_id = [i1[0:8] * PEER_NKEYS + i2[0:1], i1[8:16] * PEER_NKEYS + i2[0:1]]
    for j in range(1, 8):
        vals.append(v1[0:8] + v2[j:j + 1])
        flat.append(r8 * 16.0 + float(j))
        exp_id.append(i1[0:8] * PEER_NKEYS + i2[j:j + 1])
    vals.append(v1[0:1] + v2[8:16])
    flat.append(r8 + 8.0)
    exp_id.append(i1[0:1] * PEER_NKEYS + i2[8:16])
    cand = jnp.concatenate(vals, 0)
    pos = jnp.concatenate(flat, 0)
    eid = jnp.concatenate(exp_id, 0)
    tops, es = [], []
    for _ in range(PEER_TOPK):
        m = jnp.max(cand, axis=0, keepdims=True)
        pm = jnp.min(jnp.where(cand == m, pos, 1e9), axis=0, keepdims=True)
        hit = pos == pm
        es.append(jnp.max(jnp.where(hit, eid, -1.0), axis=0, keepdims=True))
        cand = jnp.where(hit, -jnp.inf, cand)
        tops.append(m)
    top = jnp.concatenate(tops, 0)
    ex = jnp.exp(top - top[0:1])
    off = hh * PEER_TOPK if isinstance(hh, int) else pl.multiple_of(hh * PEER_TOPK, PEER_TOPK)
    g_ref[pl.ds(off, PEER_TOPK), :] = ex / jnp.sum(ex, axis=0, keepdims=True)
    e_ref[pl.ds(off, PEER_TOPK), :] = jnp.concatenate(es, 0).astype(I32)


def _route_kernel(st_ref, g_ref, e_ref):
    def head(hh, c):
        _route_head(st_ref, g_ref, e_ref, hh)
        return c

    lax.fori_loop(0, PEER_HEADS, head, 0, unroll=2)


def _route(st):
    n = st.shape[2]
    blk = pl.BlockSpec((HK, LANES), lambda i: (0, i))
    return pl.pallas_call(
        _route_kernel,
        grid=(n // LANES,),
        in_specs=[pl.BlockSpec((2 * PEER_HEADS, PEER_NKEYS, LANES), lambda i: (0, 0, i))],
        out_specs=[blk, blk],
        out_shape=[jax.ShapeDtypeStruct((HK, n), F32), jax.ShapeDtypeStruct((HK, n), I32)],
        compiler_params=_params(("parallel",)),
        name="route",
    )(st)


EXPERT_BLOCK = 1024
EXPERT_SUB = 256


def _peer_act_kernel(x_ref, u_ref, e_ref, g_ref, w_ref, acc_scr):
    j = pl.program_id(1)

    @pl.when(j == 0)
    def _():
        acc_scr[...] = jnp.zeros_like(acc_scr)

    x = x_ref[...]
    e = e_ref[...]
    i2 = e & (PEER_NKEYS - 1)
    i1 = e >> 7
    acc = acc_scr[...]
    for c in range(EXPERT_BLOCK // EXPERT_SUB):
        hmat = lax.dot_general(x, u_ref[c * EXPERT_SUB:(c + 1) * EXPERT_SUB, :], NT,
                               preferred_element_type=F32)
        for s in range(EXPERT_SUB // PEER_NKEYS):
            got = jnp.take_along_axis(hmat[:, s * PEER_NKEYS:(s + 1) * PEER_NKEYS], i2, axis=1,
                                      mode="promise_in_bounds")
            row1 = (j * EXPERT_BLOCK + c * EXPERT_SUB) // PEER_NKEYS + s
            acc = jnp.where(i1 == row1, got, acc)
    acc_scr[...] = acc

    @pl.when(j == pl.num_programs(1) - 1)
    def _():
        a = acc_scr[...]
        gelu = 0.5 * a * (1.0 + lax.erf(a * (2.0 ** -0.5)))
        w_ref[...] = g_ref[...] * gelu


def _peer_act(xb, u, e, g, tr):
    n = xb.shape[0]
    row = lambda wd: pl.BlockSpec((tr, wd), lambda i, j: (i, 0))
    return pl.pallas_call(
        _peer_act_kernel,
        grid=(n // tr, u.shape[0] // EXPERT_BLOCK),
        in_specs=[row(D_MODEL), pl.BlockSpec((EXPERT_BLOCK, D_MODEL), lambda i, j: (j, 0)), row(HK), row(HK)],
        out_specs=row(HK),
        out_shape=jax.ShapeDtypeStruct((n, HK), F32),
        scratch_shapes=[pltpu.VMEM((tr, HK), F32)],
        compiler_params=_params(("parallel", "arbitrary")),
        name="peer_act",
    )(xb, u, e, g)


GATE_GROUP = 8


def _peer_out_kernel(e_ref, w_ref, v_ref, h1_ref, g2_ref, b2_ref, y_ref, p_scr, acc_scr):
    k = pl.program_id(1)
    tr = e_ref.shape[0]
    rows_per_step = v_ref.shape[0] // PEER_NKEYS
    sub8 = p_scr.shape[1]

    @pl.when(k == 0)
    def _():
        acc_scr[...] = jnp.zeros_like(acc_scr)
        sub = lax.broadcasted_iota(I32, (PEER_NKEYS, HK), 0)

        def body(grp, c):
            r0 = pl.multiple_of(grp * GATE_GROUP, GATE_GROUP)
            ers = e_ref[pl.ds(r0, GATE_GROUP), :]
            wrs = w_ref[pl.ds(r0, GATE_GROUP), :]
            ps = []
            for t in range(GATE_GROUP):
                er = ers[t:t + 1, :]
                wr = wrs[t:t + 1, :]
                o1 = jnp.where(sub == (er >> 7), 1.0, 0.0).astype(BF16)
                o2 = jnp.where(sub == (er & (PEER_NKEYS - 1)), wr, 0.0).astype(BF16)
                ps.append(lax.dot_general(o1, o2, NT, preferred_element_type=F32))
            x = jnp.stack(ps).reshape(GATE_GROUP, PEER_NKEYS // sub8, sub8, PEER_NKEYS)
            p_scr[:, :, pl.ds(r0, GATE_GROUP), :] = pltpu.einshape("tqsl->qstl", x)
            return c

        lax.fori_loop(0, tr // GATE_GROUP, body, 0, unroll=4)

    tot = None
    for j in range(0, rows_per_step, 2):
        q = (k * rows_per_step + j) // sub8
        lhs = jnp.concatenate([p_scr[q, j % sub8], p_scr[q, (j + 1) % sub8]], axis=1).astype(BF16)
        d = jnp.dot(lhs, v_ref[j * PEER_NKEYS:(j + 2) * PEER_NKEYS, :], preferred_element_type=F32)
        tot = d if tot is None else tot + d
    acc_scr[...] += tot

    @pl.when(k == pl.num_programs(1) - 1)
    def _():
        y_ref[...] = _ln(DN_ALPHA * h1_ref[...] + acc_scr[...]) * g2_ref[...] + b2_ref[...]


def _peer_out(e, w, v, h1, g2, b2, tr, kc):
    n = h1.shape[0]
    row = lambda wd: pl.BlockSpec((tr, wd), lambda i, k: (i, 0))
    vec = pl.BlockSpec((1, D_MODEL), lambda i, k: (0, 0))
    return pl.pallas_call(
        _peer_out_kernel,
        grid=(n // tr, v.shape[0] // kc),
        in_specs=[row(HK), row(HK), pl.BlockSpec((kc, D_MODEL), lambda i, k: (k, 0)), row(D_MODEL), vec, vec],
        out_specs=row(D_MODEL),
        out_shape=jax.ShapeDtypeStruct((n, D_MODEL), F32),
        scratch_shapes=[pltpu.VMEM((PEER_NKEYS // GATE_GROUP, GATE_GROUP, tr, PEER_NKEYS), F32),
                        pltpu.VMEM((tr, D_MODEL), F32)],
        compiler_params=_params(("parallel", "arbitrary")),
        name="peer_out",
    )(e, w, v, h1, g2, b2)


def _peer_experts(g_t, e_t, h1, h1b, wts, tr_act, tr_out, kc):
    _, _, _, g2, b2, _, _, u, v = wts
    g = g_t.T
    e = e_t.T
    w = _peer_act(h1b, u, e, g, tr_act)
    return _peer_out(e, w, v, h1, g2, b2, tr_out, kc)


def kernel(x_prompt, x_sample, cache_k, cache_v, cache_kidx, state_ret, page_table, meta_tokens, rel_bias,
           w_in, w_out, ln1_g, ln1_b, ln2_g, ln2_b, peer_wq, peer_subkeys, peer_u, peer_v):
    assert w_in.shape[0] == 1 and x_sample.shape[1] == 1
    nbp, seq, _ = x_prompt.shape
    t = seq + N_META
    nbs = x_sample.shape[0]
    n_pages = page_table.shape[1]
    page = cache_kidx.shape[2]
    past = n_pages * page
    assert page == LANES and n_pages % PAGES_PER_STEP == 0

    hp = jnp.concatenate([jnp.broadcast_to(meta_tokens[None], (nbp, N_META, D_MODEL)), x_prompt], 1)
    hp = hp.reshape(nbp * t, D_MODEL)
    hs = x_sample.reshape(nbs, D_MODEL)

    w = w_in[0]
    wa = w[:, :PROJ_MAIN].astype(BF16)
    wt = jnp.pad(w[:, PROJ_MAIN:], ((0, 0), (0, PROJ_TAIL - (w.shape[1] - PROJ_MAIN)))).astype(BF16)
    wts = (w_out[0].astype(BF16), ln1_g, ln1_b, ln2_g, ln2_b, peer_wq[0].astype(BF16),
           peer_subkeys[0].astype(BF16), peer_u[0].astype(BF16), peer_v[0].astype(BF16))
    bias_tiles, bias_rows = _bias_tables(rel_bias)

    rq, rk, rv, rg, dq, dk, dv, iq, tail = _inproj(hp, wa, wt, 384)
    b3 = lambda a: a.reshape(nbp, t, a.shape[-1])
    cosf, sins = _rot_tables(jnp.arange(t, dtype=I32))
    ret_y, ret_s = _ret_prompt(b3(rq), b3(rk), b3(rv), b3(rg), cosf, sins)
    dsa_o = _dsa_prompt(b3(dq), b3(dk), b3(dv), b3(iq), b3(tail), bias_tiles, rel_bias[REL_BUCKETS - 1])
    wo, g1, b1, _, _, wq, sk, _, _ = wts
    h1p, h1bp, stp = _tail1(ret_y.reshape(nbp * t, 512), dsa_o.reshape(nbp * t, DSA_W), hp, wo, g1, b1, wq, sk, 384)

    kidx_t = jnp.transpose(cache_kidx[0], (0, 2, 1))
    ck_t = jnp.transpose(cache_k[0], (0, 2, 3, 1))
    cv_t = jnp.transpose(cache_v[0], (0, 2, 3, 1))
    srq, srk, srv, srg, sdq, sdk, sdv, siq, stail = _inproj(hs, wa, wt, nbs)
    sret_y, sret_s = _ret_sample(srq, srk, srv, srg, state_ret[0], past)
    scores = _idx_scores(page_table, siq.reshape(nbs, DSA_HEADS, IDX_DIM),
                         stail[:, IDX_DIM:IDX_DIM + DSA_HEADS].reshape(nbs, DSA_HEADS, 1),
                         stail[:, :IDX_DIM].reshape(nbs, 1, IDX_DIM), kidx_t)
    n_sel = min(TOPK_MAX, (past + 1) // 4)
    am = _sel_mask(scores.reshape(nbs, past + page), n_sel)
    g_tp, e_tp, sdsa_o = _route_sample(stp, page_table, sdq, sdk, sdv, am, bias_rows, ck_t, cv_t)
    yp = _peer_experts(g_tp, e_tp, h1p, h1bp, wts, 1376, 384, 2048)
    y_prompt = yp.reshape(nbp, t, D_MODEL)[:, N_META:]
    h1s, h1bs, sts = _tail1(sret_y, sdsa_o.reshape(nbs, DSA_W), hs, wo, g1, b1, wq, sk, nbs)
    g_ts, e_ts = _route(sts)
    ys = _peer_experts(g_ts, e_ts, h1s, h1bs, wts, nbs, nbs, 2048)

    kv_p = lambda a: a.reshape(1, nbp, t, DSA_HEADS, DSA_DH)
    kv_s = lambda a: a.reshape(1, nbs, 1, DSA_HEADS, DSA_DH)
    return (y_prompt, ys.reshape(nbs, 1, D_MODEL), kv_p(dk), kv_p(dv),
            tail[:, :IDX_DIM].reshape(1, nbp, t, IDX_DIM), ret_s[None],
            kv_s(sdk), kv_s(sdv), stail[:, :IDX_DIM].reshape(1, nbs, 1, IDX_DIM), sret_s[None])
```

```python
import functools
import math

import numpy as np
import jax
import jax.numpy as jnp
from jax import lax
from jax.experimental import pallas as pl
from jax.experimental.pallas import tpu as pltpu

F32 = jnp.float32
BF16 = jnp.bfloat16
I32 = jnp.int32

D_MODEL = 1024
N_META = 16
RET_HEADS = 4
RET_DK = 128
DSA_HEADS = 8
DSA_DH = 64
DSA_W = DSA_HEADS * DSA_DH
IDX_DIM = 64
TOPK_MAX = 256
REL_BUCKETS = 32
REL_MAX_EXACT = 16
REL_MAX_DIST = 128
PEER_HEADS = 8
PEER_NKEYS = 128
PEER_DKEY = 256
PEER_TOPK = 16
HK = PEER_HEADS * PEER_TOPK
DN_ALPHA = 2.0 ** 0.25
LN_EPS = 1e-5
PROJ_MAIN = 4096
PROJ_TAIL = 128

LANES = 128
INT_MIN = -2 ** 31
VMEM_LIMIT = 56 * 1024 * 1024

NT = (((1,), (1,)), ((), ()))


def _params(sem, vmem=VMEM_LIMIT):
    return pltpu.CompilerParams(dimension_semantics=sem, vmem_limit_bytes=vmem)


def _ln(x):
    mu = jnp.mean(x, axis=-1, keepdims=True)
    xc = x - mu
    var = jnp.mean(xc * xc, axis=-1, keepdims=True)
    return xc * lax.rsqrt(var + LN_EPS)


def _inproj_kernel(x_ref, wa_ref, wt_ref, *out_refs):
    x = x_ref[...].astype(BF16)
    for j in range(8):
        out_refs[j][...] = jnp.dot(x, wa_ref[:, j * 512:(j + 1) * 512], preferred_element_type=F32)
    out_refs[8][...] = jnp.dot(x, wt_ref[...], preferred_element_type=F32)


def _inproj(x, wa, wt, tm):
    n = x.shape[0]
    row = lambda i: (i, 0)
    fixed = lambda i: (0, 0)
    return pl.pallas_call(
        _inproj_kernel,
        grid=(n // tm,),
        in_specs=[pl.BlockSpec((tm, D_MODEL), row), pl.BlockSpec((D_MODEL, PROJ_MAIN), fixed),
                  pl.BlockSpec((D_MODEL, PROJ_TAIL), fixed)],
        out_specs=[pl.BlockSpec((tm, 512), row)] * 8 + [pl.BlockSpec((tm, PROJ_TAIL), row)],
        out_shape=[jax.ShapeDtypeStruct((n, 512), F32)] * 8 + [jax.ShapeDtypeStruct((n, PROJ_TAIL), F32)],
        compiler_params=_params(("parallel",)),
        name="inproj",
    )(x, wa, wt)


def _ret_gammas():
    return [float(np.exp(np.log(np.float32(1.0 - 2.0 ** (-5.0 - h))))) for h in range(RET_HEADS)]


def _ret_tables(last_rows):
    c = LANES
    lg = np.log(1.0 - 2.0 ** (-5.0 - np.arange(RET_HEADS, dtype=np.float64)))
    i = np.arange(c, dtype=np.float64)
    diff = i[:, None] - i[None, :]
    dmask = np.where(diff[None] >= 0, np.exp(np.maximum(diff[None], 0.0) * lg[:, None, None]), 0.0)
    cdec = np.exp((i[None, :] + 1.0) * lg[:, None])[:, :, None] * np.ones((1, 1, c))
    kfull = np.exp((c - 1.0 - i)[None, :] * lg[:, None])
    klast = np.where(i[None, :] < last_rows, np.exp(np.maximum(last_rows - 1.0 - i, 0.0)[None, :] * lg[:, None]), 0.0)
    kdec = np.stack([kfull, klast])[:, :, :, None] * np.ones((1, 1, 1, c))
    g_full = [float(np.exp(c * l)) for l in lg]
    g_last = [float(np.exp(last_rows * l)) for l in lg]
    return (jnp.asarray(dmask, F32), jnp.asarray(cdec, F32), jnp.asarray(kdec, F32), g_full, g_last)


def _rot_tables(pos):
    half = RET_DK // 2
    inv = 1.0 / (10000.0 ** (jnp.arange(half, dtype=F32) / half))
    ang = pos.astype(F32)[:, None] * inv[None, :]
    cos, sin = jnp.cos(ang), jnp.sin(ang)
    return jnp.concatenate([cos, cos], -1), jnp.concatenate([-sin, sin], -1)


def _ret_prompt_kernel(q_ref, k_ref, v_ref, g_ref, cos_ref, sin_ref, dmask_ref, cdec_ref, kdec_ref,
                       y_ref, s_out_ref, s_scr, *, t_len, g_full, g_last):
    c = pl.program_id(1)
    is_last = c == pl.num_programs(1) - 1

    @pl.when(c == 0)
    def _():
        s_scr[...] = jnp.zeros_like(s_scr)

    row = lax.broadcasted_iota(I32, (LANES, 1), 0) + c * LANES
    valid = row < t_len
    cosf = cos_ref[...]
    sins = sin_ref[...]
    scale = RET_DK ** -0.5
    for h in range(RET_HEADS):
        sl = slice(h * RET_DK, (h + 1) * RET_DK)
        q = q_ref[:, sl]
        k = k_ref[:, sl]
        qr = jnp.where(valid, (q * cosf + pltpu.roll(q, RET_DK // 2, 1) * sins) * scale, 0.0)
        kr = jnp.where(valid, k * cosf + pltpu.roll(k, RET_DK // 2, 1) * sins, 0.0)
        v = jnp.where(valid, v_ref[:, sl], 0.0)
        qb = qr.astype(BF16)
        kb = kr.astype(BF16)
        vb = v.astype(BF16)
        sc = lax.dot_general(qb, kb, NT, preferred_element_type=F32) * dmask_ref[h]
        inner = jnp.dot(sc.astype(BF16), vb, preferred_element_type=F32)
        s_old = s_scr[h]
        cross = jnp.dot(qb, s_old.astype(BF16), preferred_element_type=F32) * cdec_ref[h]
        o = inner + cross
        kd_t = (kr * kdec_ref[h]).T.astype(BF16)
        gdec = jnp.where(is_last, g_last[h], g_full[h])
        s_scr[h] = gdec * s_old + jnp.dot(kd_t, vb, preferred_element_type=F32)
        g = g_ref[:, sl]
        y_ref[:, sl] = (_ln(o) * (g * jax.nn.sigmoid(g))).astype(BF16)

    @pl.when(is_last)
    def _():
        s_out_ref[...] = s_scr[...]


def _ret_prompt(rq, rk, rv, rg, cosf, sins):
    b, t, _ = rq.shape
    nc = pl.cdiv(t, LANES)
    last_rows = t - (nc - 1) * LANES
    dmask, cdec, kdec, g_full, g_last = _ret_tables(last_rows)
    blk = pl.BlockSpec((None, LANES, 512), lambda bi, ci: (bi, ci, 0))
    tab = pl.BlockSpec((LANES, RET_DK), lambda bi, ci: (ci, 0))
    full3 = pl.BlockSpec((RET_HEADS, LANES, LANES), lambda bi, ci: (0, 0, 0))
    return pl.pallas_call(
        functools.partial(_ret_prompt_kernel, t_len=t, g_full=g_full, g_last=g_last),
        grid=(b, nc),
        in_specs=[blk, blk, blk, blk, tab, tab, full3, full3,
                  pl.BlockSpec((None, RET_HEADS, LANES, LANES), lambda bi, ci: (ci // (nc - 1), 0, 0, 0))],
        out_specs=[blk, pl.BlockSpec((None, RET_HEADS, RET_DK, RET_DK), lambda bi, ci: (bi, 0, 0, 0))],
        out_shape=[jax.ShapeDtypeStruct((b, t, 512), BF16),
                   jax.ShapeDtypeStruct((b, RET_HEADS, RET_DK, RET_DK), F32)],
        scratch_shapes=[pltpu.VMEM((RET_HEADS, RET_DK, RET_DK), F32)],
        compiler_params=_params(("parallel", "arbitrary")),
        name="ret_prompt",
    )(rq, rk, rv, rg, cosf, sins, dmask, cdec, kdec)


def _ret_sample_kernel(q_ref, k_ref, qt_ref, kt_ref, v_ref, g_ref, cos_ref, sin_ref, cost_ref, sint_ref,
                       s0_ref, y_ref, s1_ref, o_scr, *, gammas):
    h = pl.program_id(0)
    gamma = jnp.float32(gammas[0])
    for i in range(1, RET_HEADS):
        gamma = jnp.where(h == i, jnp.float32(gammas[i]), gamma)
    scale = RET_DK ** -0.5
    half = RET_DK // 2

    def rot_t(x):
        return x * cost_ref[...] + jnp.concatenate([x[half:], x[:half]], axis=0) * sint_ref[...]

    def rot(x):
        return x * cos_ref[...] + pltpu.roll(x, half, 1) * sin_ref[...]

    def r16(x):
        return x.astype(BF16).astype(F32)

    qt = r16(rot_t(qt_ref[...]) * scale)
    kt = r16(rot_t(kt_ref[...]))
    q = r16(rot(q_ref[...]) * scale)
    k = r16(rot(k_ref[...]))
    vb = r16(v_ref[...])
    qk = r16(jnp.sum(q * k, axis=1, keepdims=True))
    inner = qk * vb
    for b in range(q.shape[0]):
        s_old = s0_ref[b]
        qc = qt[:, b:b + 1]
        kc = kt[:, b:b + 1]
        o_scr[b:b + 1, :] = jnp.sum(qc * r16(s_old), axis=0, keepdims=True) * gamma
        s1_ref[b] = gamma * s_old + kc * vb[b:b + 1, :]
    o = inner + o_scr[...]
    g = g_ref[...]
    y_ref[...] = (_ln(o) * (g * jax.nn.sigmoid(g))).astype(BF16)


def _ret_sample(rq, rk, rv, rg, state, pos):
    nb = rq.shape[0]
    cosf, sins = _rot_tables(jnp.full((1,), pos, I32))
    cos_b = jnp.broadcast_to(cosf, (nb, RET_DK))
    sin_b = jnp.broadcast_to(sins, (nb, RET_DK))
    cos_t = jnp.broadcast_to(cosf.reshape(RET_DK, 1), (RET_DK, nb))
    sin_t = jnp.broadcast_to(sins.reshape(RET_DK, 1), (RET_DK, nb))
    qt = rq.reshape(nb, RET_HEADS, RET_DK).transpose(1, 2, 0)
    kt = rk.reshape(nb, RET_HEADS, RET_DK).transpose(1, 2, 0)
    col = pl.BlockSpec((nb, RET_DK), lambda h: (0, h))
    tr = pl.BlockSpec((None, RET_DK, nb), lambda h: (h, 0, 0))
    full = pl.BlockSpec((nb, RET_DK), lambda h: (0, 0))
    full_t = pl.BlockSpec((RET_DK, nb), lambda h: (0, 0))
    st = pl.BlockSpec((nb, None, RET_DK, RET_DK), lambda h: (0, h, 0, 0))
    return pl.pallas_call(
        functools.partial(_ret_sample_kernel, gammas=_ret_gammas()),
        grid=(RET_HEADS,),
        in_specs=[col, col, tr, tr, col, col, full, full, full_t, full_t, st],
        out_specs=[col, st],
        out_shape=[jax.ShapeDtypeStruct((nb, 512), BF16), jax.ShapeDtypeStruct(state.shape, F32)],
        scratch_shapes=[pltpu.VMEM((nb, RET_DK), F32)],
        compiler_params=_params(("parallel",)),
        name="ret_sample",
    )(rq, rk, qt, kt, rv, rg, cos_b, sin_b, cos_t, sin_t, state)


def _bucket_np(d):
    d = np.maximum(d, 0)
    lb = REL_MAX_EXACT + (np.log(np.maximum(d, 1).astype(np.float32) / np.float32(REL_MAX_EXACT))
                          / np.float32(math.log(REL_MAX_DIST / REL_MAX_EXACT))
                          * np.float32(REL_BUCKETS - REL_MAX_EXACT)).astype(np.int32)
    return np.where(d < REL_MAX_EXACT, d, np.minimum(lb, REL_BUCKETS - 1)).astype(np.int32)


def _bias_kernel(rb_ref, bidx_ref, bidx_s_ref, out_ref, out_s_ref):
    for m in range(2):
        bi = bidx_ref[m]
        for h in range(DSA_HEADS):
            acc = jnp.zeros((LANES, LANES), F32)
            for b in range(REL_BUCKETS):
                acc = jnp.where(bi == b, rb_ref[b, h], acc)
            out_ref[h, m] = acc
    for m in range(3):
        bi = bidx_s_ref[m:m + 1, :]
        for h in range(DSA_HEADS):
            acc = jnp.zeros((1, LANES), F32)
            for b in range(REL_BUCKETS):
                acc = jnp.where(bi == b, rb_ref[b, h], acc)
            out_s_ref[m, h:h + 1, :] = acc


def _bias_tables(rel_bias):
    r = np.arange(LANES)
    d = r[:, None] - r[None, :]
    bidx = jnp.asarray(np.stack([_bucket_np(d), _bucket_np(d + LANES)]), I32)
    assert int(_bucket_np(np.array([LANES]))[0]) == REL_BUCKETS - 1
    bidx_s = jnp.asarray(np.stack([np.full(LANES, REL_BUCKETS - 1), _bucket_np(LANES - r), np.zeros(LANES)]), I32)
    return pl.pallas_call(
        _bias_kernel,
        in_specs=[pl.BlockSpec(memory_space=pltpu.SMEM), pl.BlockSpec(memory_space=pltpu.VMEM),
                  pl.BlockSpec(memory_space=pltpu.VMEM)],
        out_specs=[pl.BlockSpec(memory_space=pltpu.VMEM), pl.BlockSpec(memory_space=pltpu.VMEM)],
        out_shape=[jax.ShapeDtypeStruct((DSA_HEADS, 2, LANES, LANES), F32),
                   jax.ShapeDtypeStruct((3, DSA_HEADS, LANES), F32)],
        name="bias_tables",
    )(rel_bias, bidx, bidx_s)


def _order_key(x):
    bits = pltpu.bitcast(jnp.where(x == 0.0, 0.0, x), I32)
    return jnp.where(bits < 0, bits ^ jnp.int32(0x7FFFFFFF), bits)


def _count(mask):
    return jnp.sum(jnp.where(mask, 1.0, 0.0), axis=1, keepdims=True)


def _select_mask(key_ref, j0_ref, n_sel, col_bits, row_ok=None):
    rows, width = key_ref.shape

    def search(it, res):
        cand = res | lax.shift_left(jnp.int32(1), 31 - it)
        cnt = _count(key_ref[...] >= (cand ^ INT_MIN))
        return jnp.where(cnt >= n_sel, cand, res)

    thr = lax.fori_loop(0, 32, search, jnp.zeros((rows, 1), I32), unroll=8) ^ INT_MIN
    key = key_ref[...]
    need = n_sel - _count(key > thr)
    n_ge = jnp.where(thr == INT_MIN, 0.0, _count(key >= thr))
    if row_ok is not None:
        n_ge = jnp.where(row_ok, n_ge, 0.0)
    j0_ref[...] = jnp.full((rows, 1), width, I32)

    @pl.when(jnp.max(n_ge) > n_sel)
    def _():
        def tie_search(it, res):
            cand = res | lax.shift_left(jnp.int32(1), col_bits - 1 - it)
            col = lax.broadcasted_iota(I32, (rows, width), 1)
            before = jnp.sum(jnp.where(key_ref[...] == thr, jnp.where(col < cand, 1.0, 0.0), 0.0),
                             axis=1, keepdims=True)
            return jnp.where(before < need, cand, res)

        j0_ref[...] = lax.fori_loop(0, col_bits, tie_search, jnp.zeros((rows, 1), I32))

    col = lax.broadcasted_iota(I32, (rows, width), 1)
    ninf = jnp.float32(-jnp.inf)
    tie_ok = jnp.where(key == thr, jnp.where(col <= j0_ref[...], 0.0, ninf), ninf)
    return jnp.where(key > thr, 0.0, tie_ok)


def _dsa_prompt_tile(i, tkw, bfar_ref, dq_ref, iq_ref, tq_ref, bt_ref, o_ref,
                     kpad, vpad, kipad, key_scr, am_scr, lg_scr, j0_scr, *, t_len, n_sel, col_bits):
    key_v = key_scr.at[:, 0:tkw]
    am_v = am_scr.at[:, 0:tkw]
    lg_v = lg_scr.at[:, 0:tkw]

    rowpos = i * LANES + lax.broadcasted_iota(I32, (LANES, 1), 0)
    row_ok = rowpos < t_len

    qi = jnp.where(row_ok, iq_ref[...], 0.0).astype(BF16)
    w = jnp.where(row_ok, tq_ref[:, IDX_DIM:IDX_DIM + DSA_HEADS], 0.0)
    ki = kipad[0:tkw, 0:IDX_DIM]
    sc = None
    for h in range(DSA_HEADS):
        d = lax.dot_general(qi[:, h * IDX_DIM:(h + 1) * IDX_DIM], ki, NT, preferred_element_type=F32)
        t = w[:, h:h + 1] * jnp.maximum(d, 0.0)
        sc = t if sc is None else sc + t

    col = lax.broadcasted_iota(I32, (LANES, tkw), 1)
    vis = col <= rowpos
    key_v[...] = jnp.where(vis, _order_key(sc), INT_MIN)
    am_v[...] = jnp.where(vis, _select_mask(key_v, j0_scr, n_sel, col_bits, row_ok), -jnp.inf)

    q = jnp.where(row_ok, dq_ref[...], 0.0) * (DSA_DH ** -0.5)
    c0 = pl.multiple_of(i * LANES, LANES)
    c1 = pl.multiple_of(jnp.maximum(i - 1, 0) * LANES, LANES)
    for h in range(DSA_HEADS):
        hs = slice(h * DSA_DH, (h + 1) * DSA_DH)
        qh = q[:, hs].astype(BF16)
        lg_v[...] = (lax.dot_general(qh, kpad[0:tkw, hs], NT, preferred_element_type=F32)
                     + bfar_ref[h] + am_v[...])
        for m, cm in ((1, c1), (0, c0)):
            blk = lax.dot_general(qh, kpad[pl.ds(cm, LANES), hs], NT, preferred_element_type=F32)
            lg_scr[:, pl.ds(cm, LANES)] = blk + bt_ref[h, m] + am_scr[:, pl.ds(cm, LANES)]
        lg = lg_v[...]
        mx = jnp.max(lg, axis=1, keepdims=True)
        p = jnp.exp(lg - mx)
        den = jnp.sum(p, axis=1, keepdims=True)
        oh = jnp.dot(p.astype(BF16), vpad[0:tkw, hs], preferred_element_type=F32) / den
        o_ref[:, hs] = oh.astype(BF16)


def _dsa_prompt_kernel(bfar_ref, dq_ref, iq_ref, tq_ref, dk_ref, dv_ref, tk_ref, bt_ref, o_ref,
                       kpad, vpad, kipad, key_scr, am_scr, lg_scr, j0_scr, *, t_len, tk_pad, widths, n_sel, col_bits):
    i = pl.program_id(1)

    @pl.when(i == 0)
    def _():
        kpad[0:t_len] = dk_ref[...].astype(BF16)
        kpad[t_len:tk_pad] = jnp.zeros((tk_pad - t_len, DSA_W), BF16)
        vpad[0:t_len] = dv_ref[...].astype(BF16)
        vpad[t_len:tk_pad] = jnp.zeros((tk_pad - t_len, DSA_W), BF16)
        kipad[0:t_len] = tk_ref[...].astype(BF16)
        kipad[t_len:tk_pad] = jnp.zeros((tk_pad - t_len, PROJ_TAIL), BF16)

    lo = 0
    for nblk in widths:
        @pl.when(jnp.logical_and(i >= lo, i < nblk))
        def _(nblk=nblk):
            _dsa_prompt_tile(i, nblk * LANES, bfar_ref, dq_ref, iq_ref, tq_ref, bt_ref, o_ref,
                             kpad, vpad, kipad, key_scr, am_scr, lg_scr, j0_scr,
                             t_len=t_len, n_sel=n_sel, col_bits=col_bits)
        lo = nblk


def _dsa_prompt(dq, dk, dv, iq, tail, bias_tiles, bias_far):
    b, t, _ = dq.shape
    n_sel = min(TOPK_MAX, t // 4)
    nq = pl.cdiv(t, LANES)
    tk_pad = nq * LANES
    col_bits = max(1, int(math.ceil(math.log2(tk_pad + 1))))
    widths = sorted({int(math.ceil(nq * j / 4)) for j in range(1, 5)})
    qblk = lambda wd: pl.BlockSpec((None, LANES, wd), lambda bi, qi: (bi, qi, 0))
    kblk = lambda wd: pl.BlockSpec((None, t, wd), lambda bi, qi: (bi, 0, 0))
    return pl.pallas_call(
        functools.partial(_dsa_prompt_kernel, t_len=t, tk_pad=tk_pad, widths=widths, n_sel=n_sel, col_bits=col_bits),
        grid=(b, nq),
        in_specs=[pl.BlockSpec(memory_space=pltpu.SMEM), qblk(DSA_W), qblk(DSA_W), qblk(PROJ_TAIL),
                  kblk(DSA_W), kblk(DSA_W), kblk(PROJ_TAIL),
                  pl.BlockSpec((DSA_HEADS, 2, LANES, LANES), lambda bi, qi: (0, 0, 0, 0))],
        out_specs=qblk(DSA_W),
        out_shape=jax.ShapeDtypeStruct((b, t, DSA_W), BF16),
        scratch_shapes=[pltpu.VMEM((tk_pad, DSA_W), BF16), pltpu.VMEM((tk_pad, DSA_W), BF16),
                        pltpu.VMEM((tk_pad, PROJ_TAIL), BF16), pltpu.VMEM((LANES, tk_pad), I32),
                        pltpu.VMEM((LANES, tk_pad), F32), pltpu.VMEM((LANES, tk_pad), F32),
                        pltpu.VMEM((LANES, 1), I32)],
        compiler_params=_params(("parallel", "arbitrary")),
        name="dsa_prompt",
    )(bias_far, dq, iq, tail, dk, dv, tail, bias_tiles)


def _idx_scores_kernel(pt_ref, qi_ref, w_ref, kin_ref, kidx_hbm, out_ref, buf, sem, *, n_pages, page, past):
    b = pl.program_id(0)
    slot = b % 2

    def page_copy(sl, p, src_page):
        dst = buf.at[sl, :, pl.ds(pl.multiple_of(p * page, page), page)]
        return pltpu.make_async_copy(kidx_hbm.at[src_page], dst, sem.at[sl])

    def start_all(bb, sl):
        def start(p, c):
            page_copy(sl, p, pt_ref[bb, p]).start()
            return c

        lax.fori_loop(0, n_pages, start, 0)

    @pl.when(b == 0)
    def _():
        start_all(b, slot)

    @pl.when(b + 1 < pl.num_programs(0))
    def _():
        start_all(b + 1, 1 - slot)

    def wait(p, c):
        page_copy(slot, p, 0).wait()
        return c

    lax.fori_loop(0, n_pages, wait, 0)
    qi = qi_ref[...].astype(BF16)
    w = w_ref[...]
    d = jnp.dot(qi, buf[slot].astype(BF16), preferred_element_type=F32)
    out_ref[:, 0:past] = jnp.sum(w * jnp.maximum(d, 0.0), axis=0, keepdims=True)
    kn = kin_ref[...].astype(BF16).astype(F32)
    dn = jnp.sum(qi.astype(F32) * kn, axis=1, keepdims=True)
    sn = jnp.sum(w * jnp.maximum(dn, 0.0), axis=0, keepdims=True)
    lane = lax.broadcasted_iota(I32, (1, page), 1)
    out_ref[:, past:past + page] = jnp.where(lane == 0, sn, -jnp.inf)


def _idx_scores(page_table, qi, w, ki_new, kidx_t):
    nb, n_pages = page_table.shape
    page = kidx_t.shape[2]
    past = n_pages * page
    width = past + page
    grid_spec = pltpu.PrefetchScalarGridSpec(
        num_scalar_prefetch=1,
        grid=(nb,),
        in_specs=[pl.BlockSpec((None, DSA_HEADS, IDX_DIM), lambda b, pt: (b, 0, 0)),
                  pl.BlockSpec((None, DSA_HEADS, 1), lambda b, pt: (b, 0, 0)),
                  pl.BlockSpec((None, 1, IDX_DIM), lambda b, pt: (b, 0, 0)),
                  pl.BlockSpec(memory_space=pl.ANY)],
        out_specs=pl.BlockSpec((None, 1, width), lambda b, pt: (b, 0, 0)),
        scratch_shapes=[pltpu.VMEM((2, IDX_DIM, past), F32), pltpu.SemaphoreType.DMA((2,))],
    )
    return pl.pallas_call(
        functools.partial(_idx_scores_kernel, n_pages=n_pages, page=page, past=past),
        grid_spec=grid_spec,
        out_shape=jax.ShapeDtypeStruct((nb, 1, width), F32),
        compiler_params=_params(("arbitrary",)),
        name="idx_scores",
    )(page_table, qi, w, ki_new, kidx_t)


def _sel_mask_kernel(s_ref, am_ref, key_scr, j0_scr, *, n_sel, col_bits):
    key_scr[...] = _order_key(s_ref[...])
    am_ref[...] = _select_mask(key_scr, j0_scr, n_sel, col_bits)


def _sel_mask(scores, n_sel):
    nb, width = scores.shape
    col_bits = max(1, int(math.ceil(math.log2(width + 1))))
    return pl.pallas_call(
        functools.partial(_sel_mask_kernel, n_sel=n_sel, col_bits=col_bits),
        in_specs=[pl.BlockSpec(memory_space=pltpu.VMEM)],
        out_specs=pl.BlockSpec(memory_space=pltpu.VMEM),
        out_shape=jax.ShapeDtypeStruct((nb, width), F32),
        scratch_shapes=[pltpu.VMEM((nb, width), I32), pltpu.VMEM((nb, 1), I32)],
        compiler_params=pltpu.CompilerParams(vmem_limit_bytes=VMEM_LIMIT),
        name="sel_mask",
    )(scores)


PAGES_PER_STEP = 16
PAGE_SLOTS = 3


def _hrows(h):
    return slice(h * DSA_DH, (h + 1) * DSA_DH)


def _eye_dh():
    return lax.broadcasted_iota(I32, (DSA_DH, DSA_DH), 0) == lax.broadcasted_iota(I32, (DSA_DH, DSA_DH), 1)


MXU_ROWS = 16


def _sample_begin(q_ref, qb_scr, m_scr, l_scr, acc_scr):
    q = q_ref[...] * (DSA_DH ** -0.5)
    head_of_col = lax.broadcasted_iota(I32, (MXU_ROWS, DSA_W), 1) // DSA_DH
    row = lax.broadcasted_iota(I32, (MXU_ROWS, DSA_W), 0)
    qb_scr[...] = jnp.where(head_of_col == row, jnp.broadcast_to(q, (MXU_ROWS, DSA_W)), 0.0)
    m_scr[...] = jnp.full(m_scr.shape, -1e30, F32)
    l_scr[...] = jnp.zeros_like(l_scr)
    acc_scr[...] = jnp.zeros_like(acc_scr)


def _sample_chunk(kchunk, vchunk, am_row, bs_ref, last_chunk, qb_scr, m_scr, l_scr, acc_scr):
    pps = kchunk.shape[0]
    qb = qb_scr[...].astype(BF16)
    lgs = []
    for i in range(pps):
        lg_i = jnp.dot(qb, kchunk[i].reshape(DSA_W, LANES).astype(BF16), preferred_element_type=F32)[0:DSA_HEADS]
        lgs.append(lg_i + (bs_ref[1] if last_chunk and i == pps - 1 else bs_ref[0]))
    lg = jnp.concatenate(lgs, axis=1) + am_row
    m_old = m_scr[:, 0:1]
    m_new = jnp.maximum(m_old, jnp.max(lg, axis=1, keepdims=True))
    alpha = jnp.exp(m_old - m_new)
    p = jnp.exp(lg - m_new)
    l_scr[...] = jnp.broadcast_to(l_scr[:, 0:1] * alpha + jnp.sum(p, axis=1, keepdims=True), l_scr.shape)
    m_scr[...] = jnp.broadcast_to(m_new, m_scr.shape)
    vcat = jnp.concatenate([vchunk[i].reshape(DSA_W, LANES).astype(BF16) for i in range(pps)], axis=1)
    p16 = jnp.concatenate([p, jnp.zeros((MXU_ROWS - DSA_HEADS, p.shape[1]), F32)], axis=0).astype(BF16)
    pv = lax.dot_general(vcat, p16, NT, preferred_element_type=F32)
    alpha_rows = jnp.concatenate([jnp.broadcast_to(alpha[h:h + 1, :], (DSA_DH, 1)) for h in range(DSA_HEADS)], axis=0)
    acc_scr[...] = acc_scr[...] * alpha_rows + pv


def _sample_finish(q_ref, kn_ref, vn_ref, am_new, bs_ref, o_ref, m_scr, l_scr, acc_scr):
    q = q_ref[...] * (DSA_DH ** -0.5)
    kn = kn_ref[...]
    vn = vn_ref[...]
    outs = []
    for h in range(DSA_HEADS):
        lgn = (jnp.sum(q[:, _hrows(h)] * kn[:, _hrows(h)], axis=1, keepdims=True)
               + bs_ref[2][h:h + 1, 0:1] + am_new)
        mo = m_scr[h:h + 1, 0:1]
        mn = jnp.maximum(mo, lgn)
        al = jnp.exp(mo - mn)
        pn = jnp.exp(lgn - mn)
        den = l_scr[h:h + 1, 0:1] * al + pn
        ocol = acc_scr[_hrows(h), h:h + 1]
        orow = jnp.sum(jnp.where(_eye_dh(), jnp.broadcast_to(ocol, (DSA_DH, DSA_DH)), 0.0), axis=0, keepdims=True)
        outs.append((orow * al + pn * vn[:, _hrows(h)]) / den)
    o_ref[...] = jnp.concatenate(outs, axis=1).astype(BF16)


def _route_sample_kernel(pt_sm, st_ref, q_ref, kn_ref, vn_ref, am_ref, bs_ref, ck_hbm, cv_hbm,
                         g_ref, e_ref, o_ref, kbuf, vbuf, sem, qb_scr, m_scr, l_scr, acc_scr, *, nb, n_chunks):
    pps = PAGES_PER_STEP
    t = pl.program_id(0)
    n_steps = pl.num_programs(0)
    heads_per_chunk = PEER_HEADS // n_chunks
    past = n_chunks * pps * LANES

    n_slots = kbuf.shape[0]
    ahead = n_slots - 1

    def slot_of(step, c):
        return (step * n_chunks + c) % n_slots

    def chunk_copies(step, c, lookup):
        sl = slot_of(step, c)
        cps = []
        for i in range(pps):
            pg = pt_sm[jnp.minimum(step, nb - 1), c * pps + i] if lookup else 0
            cps.append(pltpu.make_async_copy(ck_hbm.at[pg], kbuf.at[sl, i], sem.at[0, sl]))
            cps.append(pltpu.make_async_copy(cv_hbm.at[pg], vbuf.at[sl, i], sem.at[1, sl]))
        return cps

    @pl.when(t == 0)
    def _():
        for c in range(ahead):
            for cp in chunk_copies(t, c, True):
                cp.start()

    _sample_begin(q_ref, qb_scr, m_scr, l_scr, acc_scr)
    for c in range(n_chunks):
        if c + ahead < n_chunks:
            for cp in chunk_copies(t, c + ahead, True):
                cp.start()
        else:
            @pl.when(t + 1 < n_steps)
            def _(c=c):
                for cp in chunk_copies(t + 1, c + ahead - n_chunks, True):
                    cp.start()
        for cp in chunk_copies(t, c, False):
            cp.wait()
        sl = slot_of(t, c)
        _sample_chunk(kbuf.at[sl], vbuf.at[sl], am_ref[:, c * pps * LANES:(c + 1) * pps * LANES], bs_ref,
                      c == n_chunks - 1, qb_scr, m_scr, l_scr, acc_scr)
        for hh in range(c * heads_per_chunk, (c + 1) * heads_per_chunk):
            _route_head(st_ref, g_ref, e_ref, hh)
    _sample_finish(q_ref, kn_ref, vn_ref, am_ref[:, past:past + 1], bs_ref, o_ref, m_scr, l_scr, acc_scr)


def _route_sample(st, page_table, dq, dk_new, dv_new, am, bias_rows, ck_t, cv_t):
    n = st.shape[2]
    n_tiles = n // LANES
    nb, n_pages = page_table.shape
    page = ck_t.shape[3]
    pps = PAGES_PER_STEP
    n_chunks = n_pages // pps
    assert PEER_HEADS % n_chunks == 0 and PAGE_SLOTS - 1 <= n_chunks
    n_steps = max(n_tiles, nb)
    tile = lambda t, pt: jnp.minimum(t, n_tiles - 1)
    row = lambda wd: pl.BlockSpec((None, 1, wd), lambda t, pt: (jnp.minimum(t, nb - 1), 0, 0))
    blk = pl.BlockSpec((HK, LANES), lambda t, pt: (0, tile(t, pt)))
    grid_spec = pltpu.PrefetchScalarGridSpec(
        num_scalar_prefetch=1,
        grid=(n_steps,),
        in_specs=[pl.BlockSpec((2 * PEER_HEADS, PEER_NKEYS, LANES), lambda t, pt: (0, 0, tile(t, pt))),
                  row(DSA_W), row(DSA_W), row(DSA_W), row(am.shape[-1]),
                  pl.BlockSpec((3, DSA_HEADS, LANES), lambda t, pt: (0, 0, 0)),
                  pl.BlockSpec(memory_space=pl.ANY), pl.BlockSpec(memory_space=pl.ANY)],
        out_specs=[blk, blk, row(DSA_W)],
        scratch_shapes=[pltpu.VMEM((PAGE_SLOTS, pps, DSA_HEADS, DSA_DH, page), F32),
                        pltpu.VMEM((PAGE_SLOTS, pps, DSA_HEADS, DSA_DH, page), F32),
                        pltpu.SemaphoreType.DMA((2, PAGE_SLOTS)),
                        pltpu.VMEM((MXU_ROWS, DSA_W), F32), pltpu.VMEM((DSA_HEADS, LANES), F32),
                        pltpu.VMEM((DSA_HEADS, LANES), F32), pltpu.VMEM((DSA_W, MXU_ROWS), F32)],
    )
    r3 = lambda a: a.reshape(nb, 1, a.shape[-1])
    return pl.pallas_call(
        functools.partial(_route_sample_kernel, nb=nb, n_chunks=n_chunks),
        grid_spec=grid_spec,
        out_shape=[jax.ShapeDtypeStruct((HK, n), F32), jax.ShapeDtypeStruct((HK, n), I32),
                   jax.ShapeDtypeStruct((nb, 1, DSA_W), BF16)],
        compiler_params=_params(("arbitrary",)),
        name="route_sample",
    )(page_table, st, r3(dq), r3(dk_new), r3(dv_new), r3(am), bias_rows, ck_t, cv_t)


def _tail1_kernel(ry_ref, do_ref, h_ref, wo_ref, g1_ref, b1_ref, wq_ref, sk_ref, h1_ref, h1b_ref, st_ref):
    half = wo_ref.shape[0] // 2
    mix = (jnp.dot(ry_ref[...], wo_ref[0:half], preferred_element_type=F32)
           + jnp.dot(do_ref[...], wo_ref[half:], preferred_element_type=F32))
    h1 = _ln(DN_ALPHA * h_ref[...] + mix) * g1_ref[...] + b1_ref[...]
    h1_ref[...] = h1
    h1b = h1.astype(BF16)
    h1b_ref[...] = h1b
    q = jnp.dot(h1b, wq_ref[...], preferred_element_type=F32)
    for hh in range(PEER_HEADS):
        qh = _ln(q[:, hh * PEER_DKEY:(hh + 1) * PEER_DKEY]).astype(BF16)
        for s in range(2):
            qs = qh[:, s * (PEER_DKEY // 2):(s + 1) * (PEER_DKEY // 2)]
            st_ref[hh * 2 + s] = lax.dot_general(sk_ref[hh, s], qs, NT, preferred_element_type=F32)


def _tail1(ret_y, dsa_o, h, wo, g1, b1, wq, sk, tm):
    n = h.shape[0]
    row = lambda wd: pl.BlockSpec((tm, wd), lambda i: (i, 0))
    fixed = lambda shp: pl.BlockSpec(shp, lambda i: (0,) * len(shp))
    return pl.pallas_call(
        _tail1_kernel,
        grid=(n // tm,),
        in_specs=[row(512), row(512), row(D_MODEL), fixed(wo.shape), fixed((1, D_MODEL)), fixed((1, D_MODEL)),
                  fixed(wq.shape), fixed(sk.shape)],
        out_specs=[row(D_MODEL), row(D_MODEL),
                   pl.BlockSpec((2 * PEER_HEADS, PEER_NKEYS, tm), lambda i: (0, 0, i))],
        out_shape=[jax.ShapeDtypeStruct((n, D_MODEL), F32), jax.ShapeDtypeStruct((n, D_MODEL), BF16),
                   jax.ShapeDtypeStruct((2 * PEER_HEADS, PEER_NKEYS, n), F32)],
        compiler_params=_params(("parallel",)),
        name="tail1",
    )(ret_y, dsa_o, h, wo, g1, b1, wq, sk)


def _top16(s):
    kio = lax.broadcasted_iota(I32, s.shape, 0).astype(F32)
    vals, idxs = [], []
    for _ in range(PEER_TOPK):
        m = jnp.max(s, axis=0, keepdims=True)
        first = jnp.min(jnp.where(s == m, kio, float(s.shape[0])), axis=0, keepdims=True)
        s = jnp.where(kio == first, -jnp.inf, s)
        vals.append(m)
        idxs.append(first)
    return jnp.concatenate(vals, 0), jnp.concatenate(idxs, 0)


def _route_head(st_ref, g_ref, e_ref, hh):
    r8 = lax.broadcasted_iota(I32, (8, LANES), 0).astype(F32)
    v1, i1 = _top16(st_ref[2 * hh])
    v2, i2 = _top16(st_ref[2 * hh + 1])
    vals = [v1[0:8] + v2[0:1], v1[8:16] + v2[0:1]]
    flat = [r8 * 16.0, (r8 + 8.0) * 16.0]
    exp_id = [i1[0:8] * PEER_NKEYS + i2[0:1], i1[8:16] * PEER_NKEYS + i2[0:1]]
    for j in range(1, 8):
        vals.append(v1[0:8] + v2[j:j + 1])
        flat.append(r8 * 16.0 + float(j))
        exp_id.append(i1[0:8] * PEER_NKEYS + i2[j:j + 1])
    vals.append(v1[0:1] + v2[8:16])
    flat.append(r8 + 8.0)
    exp_id.append(i1[0:1] * PEER_NKEYS + i2[8:16])
    cand = jnp.concatenate(vals, 0)
    pos = jnp.concatenate(flat, 0)
    eid = jnp.concatenate(exp_id, 0)
    tops, es = [], []
    for _ in range(PEER_TOPK):
        m = jnp.max(cand, axis=0, keepdims=True)
        pm = jnp.min(jnp.where(cand == m, pos, 1e9), axis=0, keepdims=True)
        hit = pos == pm
        es.append(jnp.max(jnp.where(hit, eid, -1.0), axis=0, keepdims=True))
        cand = jnp.where(hit, -jnp.inf, cand)
        tops.append(m)
    top = jnp.concatenate(tops, 0)
    ex = jnp.exp(top - top[0:1])
    off = hh * PEER_TOPK if isinstance(hh, int) else pl.multiple_of(hh * PEER_TOPK, PEER_TOPK)
    g_ref[pl.ds(off, PEER_TOPK), :] = ex / jnp.sum(ex, axis=0, keepdims=True)
    e_ref[pl.ds(off, PEER_TOPK), :] = jnp.concatenate(es, 0).astype(I32)


def _route_kernel(st_ref, g_ref, e_ref):
    def head(hh, c):
        _route_head(st_ref, g_ref, e_ref, hh)
        return c

    lax.fori_loop(0, PEER_HEADS, head, 0, unroll=2)


def _route(st):
    n = st.shape[2]
    blk = pl.BlockSpec((HK, LANES), lambda i: (0, i))
    return pl.pallas_call(
        _route_kernel,
        grid=(n // LANES,),
        in_specs=[pl.BlockSpec((2 * PEER_HEADS, PEER_NKEYS, LANES), lambda i: (0, 0, i))],
        out_specs=[blk, blk],
        out_shape=[jax.ShapeDtypeStruct((HK, n), F32), jax.ShapeDtypeStruct((HK, n), I32)],
        compiler_params=_params(("parallel",)),
        name="route",
    )(st)


EXPERT_BLOCK = 1024
EXPERT_SUB = 256


def _peer_act_kernel(x_ref, u_ref, e_ref, g_ref, w_ref, acc_scr):
    j = pl.program_id(1)

    @pl.when(j == 0)
    def _():
        acc_scr[...] = jnp.zeros_like(acc_scr)

    x = x_ref[...]
    e = e_ref[...]
    i2 = e & (PEER_NKEYS - 1)
    i1 = e >> 7
    acc = acc_scr[...]
    for c in range(EXPERT_BLOCK // EXPERT_SUB):
        hmat = lax.dot_general(x, u_ref[c * EXPERT_SUB:(c + 1) * EXPERT_SUB, :].astype(BF16), NT,
                               preferred_element_type=F32)
        for s in range(EXPERT_SUB // PEER_NKEYS):
            got = jnp.take_along_axis(hmat[:, s * PEER_NKEYS:(s + 1) * PEER_NKEYS], i2, axis=1,
                                      mode="promise_in_bounds")
            row1 = (j * EXPERT_BLOCK + c * EXPERT_SUB) // PEER_NKEYS + s
            acc = jnp.where(i1 == row1, got, acc)
    acc_scr[...] = acc

    @pl.when(j == pl.num_programs(1) - 1)
    def _():
        a = acc_scr[...]
        gelu = 0.5 * a * (1.0 + lax.erf(a * (2.0 ** -0.5)))
        w_ref[...] = g_ref[...] * gelu


def _peer_act(xb, u, e, g, tr):
    n = xb.shape[0]
    row = lambda wd: pl.BlockSpec((tr, wd), lambda i, j: (i, 0))
    return pl.pallas_call(
        _peer_act_kernel,
        grid=(n // tr, u.shape[0] // EXPERT_BLOCK),
        in_specs=[row(D_MODEL), pl.BlockSpec((EXPERT_BLOCK, D_MODEL), lambda i, j: (j, 0)), row(HK), row(HK)],
        out_specs=row(HK),
        out_shape=jax.ShapeDtypeStruct((n, HK), F32),
        scratch_shapes=[pltpu.VMEM((tr, HK), F32)],
        compiler_params=_params(("parallel", "arbitrary")),
        name="peer_act",
    )(xb, u, e, g)


GATE_GROUP = 8


def _peer_out_kernel(e_ref, w_ref, v_ref, h1_ref, g2_ref, b2_ref, y_ref, p_scr, acc_scr):
    k = pl.program_id(1)
    tr = e_ref.shape[0]
    rows_per_step = v_ref.shape[0] // PEER_NKEYS
    sub8 = p_scr.shape[1]

    @pl.when(k == 0)
    def _():
        acc_scr[...] = jnp.zeros_like(acc_scr)
        sub = lax.broadcasted_iota(I32, (PEER_NKEYS, HK), 0)

        def body(grp, c):
            r0 = pl.multiple_of(grp * GATE_GROUP, GATE_GROUP)
            ers = e_ref[pl.ds(r0, GATE_GROUP), :]
            wrs = w_ref[pl.ds(r0, GATE_GROUP), :]
            ps = []
            for t in range(GATE_GROUP):
                er = ers[t:t + 1, :]
                wr = wrs[t:t + 1, :]
                o1 = jnp.where(sub == (er >> 7), 1.0, 0.0).astype(BF16)
                o2 = jnp.where(sub == (er & (PEER_NKEYS - 1)), wr, 0.0).astype(BF16)
                ps.append(lax.dot_general(o1, o2, NT, preferred_element_type=F32))
            x = jnp.stack(ps).reshape(GATE_GROUP, PEER_NKEYS // sub8, sub8, PEER_NKEYS)
            p_scr[:, :, pl.ds(r0, GATE_GROUP), :] = pltpu.einshape("tqsl->qstl", x)
            return c

        lax.fori_loop(0, tr // GATE_GROUP, body, 0, unroll=4)

    tot = None
    for j in range(0, rows_per_step, 2):
        q = (k * rows_per_step + j) // sub8
        lhs = jnp.concatenate([p_scr[q, j % sub8], p_scr[q, (j + 1) % sub8]], axis=1).astype(BF16)
        d = jnp.dot(lhs, v_ref[j * PEER_NKEYS:(j + 2) * PEER_NKEYS, :], preferred_element_type=F32)
        tot = d if tot is None else tot + d
    acc_scr[...] += tot

    @pl.when(k == pl.num_programs(1) - 1)
    def _():
        y_ref[...] = _ln(DN_ALPHA * h1_ref[...] + acc_scr[...]) * g2_ref[...] + b2_ref[...]


def _peer_out(e, w, v, h1, g2, b2, tr, kc):
    n = h1.shape[0]
    row = lambda wd: pl.BlockSpec((tr, wd), lambda i, k: (i, 0))
    vec = pl.BlockSpec((1, D_MODEL), lambda i, k: (0, 0))
    return pl.pallas_call(
        _peer_out_kernel,
        grid=(n // tr, v.shape[0] // kc),
        in_specs=[row(HK), row(HK), pl.BlockSpec((kc, D_MODEL), lambda i, k: (k, 0)), row(D_MODEL), vec, vec],
        out_specs=row(D_MODEL),
        out_shape=jax.ShapeDtypeStruct((n, D_MODEL), F32),
        scratch_shapes=[pltpu.VMEM((PEER_NKEYS // GATE_GROUP, GATE_GROUP, tr, PEER_NKEYS), F32),
                        pltpu.VMEM((tr, D_MODEL), F32)],
        compiler_params=_params(("parallel", "arbitrary")),
        name="peer_out",
    )(e, w, v, h1, g2, b2)


def _peer_experts(g_t, e_t, h1, h1b, wts, tr_act, tr_out, kc):
    _, _, _, g2, b2, _, _, u, v = wts
    g = g_t.T
    e = e_t.T
    w = _peer_act(h1b, u, e, g, tr_act)
    return _peer_out(e, w, v, h1, g2, b2, tr_out, kc)


def kernel(x_prompt, x_sample, cache_k, cache_v, cache_kidx, state_ret, page_table, meta_tokens, rel_bias,
           w_in, w_out, ln1_g, ln1_b, ln2_g, ln2_b, peer_wq, peer_subkeys, peer_u, peer_v):
    assert w_in.shape[0] == 1 and x_sample.shape[1] == 1
    nbp, seq, _ = x_prompt.shape
    t = seq + N_META
    nbs = x_sample.shape[0]
    n_pages = page_table.shape[1]
    page = cache_kidx.shape[2]
    past = n_pages * page
    assert page == LANES and n_pages % PAGES_PER_STEP == 0

    hp = jnp.concatenate([jnp.broadcast_to(meta_tokens[None], (nbp, N_META, D_MODEL)), x_prompt], 1)
    hp = hp.reshape(nbp * t, D_MODEL)
    hs = x_sample.reshape(nbs, D_MODEL)

    w = w_in[0]
    wa = w[:, :PROJ_MAIN].astype(BF16)
    wt = jnp.pad(w[:, PROJ_MAIN:], ((0, 0), (0, PROJ_TAIL - (w.shape[1] - PROJ_MAIN)))).astype(BF16)
    wts = (w_out[0].astype(BF16), ln1_g, ln1_b, ln2_g, ln2_b, peer_wq[0].astype(BF16),
           peer_subkeys[0].astype(BF16), peer_u[0], peer_v[0].astype(BF16))
    bias_tiles, bias_rows = _bias_tables(rel_bias)

    rq, rk, rv, rg, dq, dk, dv, iq, tail = _inproj(hp, wa, wt, 384)
    b3 = lambda a: a.reshape(nbp, t, a.shape[-1])
    cosf, sins = _rot_tables(jnp.arange(t, dtype=I32))
    ret_y, ret_s = _ret_prompt(b3(rq), b3(rk), b3(rv), b3(rg), cosf, sins)
    dsa_o = _dsa_prompt(b3(dq), b3(dk), b3(dv), b3(iq), b3(tail), bias_tiles, rel_bias[REL_BUCKETS - 1])
    wo, g1, b1, _, _, wq, sk, _, _ = wts
    h1p, h1bp, stp = _tail1(ret_y.reshape(nbp * t, 512), dsa_o.reshape(nbp * t, DSA_W), hp, wo, g1, b1, wq, sk, 384)

    kidx_t = jnp.transpose(cache_kidx[0], (0, 2, 1))
    ck_t = jnp.transpose(cache_k[0], (0, 2, 3, 1))
    cv_t = jnp.transpose(cache_v[0], (0, 2, 3, 1))
    srq, srk, srv, srg, sdq, sdk, sdv, siq, stail = _inproj(hs, wa, wt, nbs)
    sret_y, sret_s = _ret_sample(srq, srk, srv, srg, state_ret[0], past)
    scores = _idx_scores(page_table, siq.reshape(nbs, DSA_HEADS, IDX_DIM),
                         stail[:, IDX_DIM:IDX_DIM + DSA_HEADS].reshape(nbs, DSA_HEADS, 1),
                         stail[:, :IDX_DIM].reshape(nbs, 1, IDX_DIM), kidx_t)
    n_sel = min(TOPK_MAX, (past + 1) // 4)
    am = _sel_mask(scores.reshape(nbs, past + page), n_sel)
    g_tp, e_tp, sdsa_o = _route_sample(stp, page_table, sdq, sdk, sdv, am, bias_rows, ck_t, cv_t)
    yp = _peer_experts(g_tp, e_tp, h1p, h1bp, wts, 1376, 384, 4096)
    y_prompt = yp.reshape(nbp, t, D_MODEL)[:, N_META:]
    h1s, h1bs, sts = _tail1(sret_y, sdsa_o.reshape(nbs, DSA_W), hs, wo, g1, b1, wq, sk, nbs)
    g_ts, e_ts = _route(sts)
    ys = _peer_experts(g_ts, e_ts, h1s, h1bs, wts, nbs, nbs, 2048)

    kv_p = lambda a: a.reshape(1, nbp, t, DSA_HEADS, DSA_DH)
    kv_s = lambda a: a.reshape(1, nbs, 1, DSA_HEADS, DSA_DH)
    return (y_prompt, ys.reshape(nbs, 1, D_MODEL), kv_p(dk), kv_p(dv),
            tail[:, :IDX_DIM].reshape(1, nbp, t, IDX_DIM), ret_s[None],
            kv_s(sdk), kv_s(sdv), stail[:, :IDX_DIM].reshape(1, nbs, 1, IDX_DIM), sret_s[None])
```

```python
import functools
import math

import numpy as np
import jax
import jax.numpy as jnp
from jax import lax
from jax.experimental import pallas as pl
from jax.experimental.pallas import tpu as pltpu

F32 = jnp.float32
BF16 = jnp.bfloat16
I32 = jnp.int32

D_MODEL = 1024
N_META = 16
RET_HEADS = 4
RET_DK = 128
DSA_HEADS = 8
DSA_DH = 64
DSA_W = DSA_HEADS * DSA_DH
IDX_DIM = 64
TOPK_MAX = 256
REL_BUCKETS = 32
REL_MAX_EXACT = 16
REL_MAX_DIST = 128
PEER_HEADS = 8
PEER_NKEYS = 128
PEER_DKEY = 256
PEER_TOPK = 16
HK = PEER_HEADS * PEER_TOPK
DN_ALPHA = 2.0 ** 0.25
LN_EPS = 1e-5
PROJ_MAIN = 4096
PROJ_TAIL = 128

LANES = 128
INT_MIN = -2 ** 31
VMEM_LIMIT = 56 * 1024 * 1024

NT = (((1,), (1,)), ((), ()))


def _params(sem, vmem=VMEM_LIMIT):
    return pltpu.CompilerParams(dimension_semantics=sem, vmem_limit_bytes=vmem)


def _ln(x):
    mu = jnp.mean(x, axis=-1, keepdims=True)
    xc = x - mu
    var = jnp.mean(xc * xc, axis=-1, keepdims=True)
    return xc * lax.rsqrt(var + LN_EPS)


def _inproj_kernel(x_ref, wa_ref, wt_ref, *out_refs):
    x = x_ref[...].astype(BF16)
    for j in range(8):
        out_refs[j][...] = jnp.dot(x, wa_ref[:, j * 512:(j + 1) * 512], preferred_element_type=F32)
    out_refs[8][...] = jnp.dot(x, wt_ref[...], preferred_element_type=F32)


def _inproj(x, wa, wt, tm):
    n = x.shape[0]
    row = lambda i: (i, 0)
    fixed = lambda i: (0, 0)
    return pl.pallas_call(
        _inproj_kernel,
        grid=(n // tm,),
        in_specs=[pl.BlockSpec((tm, D_MODEL), row), pl.BlockSpec((D_MODEL, PROJ_MAIN), fixed),
                  pl.BlockSpec((D_MODEL, PROJ_TAIL), fixed)],
        out_specs=[pl.BlockSpec((tm, 512), row)] * 8 + [pl.BlockSpec((tm, PROJ_TAIL), row)],
        out_shape=[jax.ShapeDtypeStruct((n, 512), F32)] * 8 + [jax.ShapeDtypeStruct((n, PROJ_TAIL), F32)],
        compiler_params=_params(("parallel",)),
        name="inproj",
    )(x, wa, wt)


def _ret_gammas():
    return [float(np.exp(np.log(np.float32(1.0 - 2.0 ** (-5.0 - h))))) for h in range(RET_HEADS)]


def _ret_tables(last_rows):
    c = LANES
    lg = np.log(1.0 - 2.0 ** (-5.0 - np.arange(RET_HEADS, dtype=np.float64)))
    i = np.arange(c, dtype=np.float64)
    diff = i[:, None] - i[None, :]
    dmask = np.where(diff[None] >= 0, np.exp(np.maximum(diff[None], 0.0) * lg[:, None, None]), 0.0)
    cdec = np.exp((i[None, :] + 1.0) * lg[:, None])[:, :, None] * np.ones((1, 1, c))
    kfull = np.exp((c - 1.0 - i)[None, :] * lg[:, None])
    klast = np.where(i[None, :] < last_rows, np.exp(np.maximum(last_rows - 1.0 - i, 0.0)[None, :] * lg[:, None]), 0.0)
    kdec = np.stack([kfull, klast])[:, :, :, None] * np.ones((1, 1, 1, c))
    g_full = [float(np.exp(c * l)) for l in lg]
    g_last = [float(np.exp(last_rows * l)) for l in lg]
    return (jnp.asarray(dmask, F32), jnp.asarray(cdec, F32), jnp.asarray(kdec, F32), g_full, g_last)


def _rot_tables(pos):
    half = RET_DK // 2
    inv = 1.0 / (10000.0 ** (jnp.arange(half, dtype=F32) / half))
    ang = pos.astype(F32)[:, None] * inv[None, :]
    cos, sin = jnp.cos(ang), jnp.sin(ang)
    return jnp.concatenate([cos, cos], -1), jnp.concatenate([-sin, sin], -1)


def _ret_prompt_kernel(q_ref, k_ref, v_ref, g_ref, cos_ref, sin_ref, dmask_ref, cdec_ref, kdec_ref,
                       y_ref, s_out_ref, s_scr, *, t_len, g_full, g_last):
    c = pl.program_id(1)
    is_last = c == pl.num_programs(1) - 1

    @pl.when(c == 0)
    def _():
        s_scr[...] = jnp.zeros_like(s_scr)

    row = lax.broadcasted_iota(I32, (LANES, 1), 0) + c * LANES
    valid = row < t_len
    cosf = cos_ref[...]
    sins = sin_ref[...]
    scale = RET_DK ** -0.5
    for h in range(RET_HEADS):
        sl = slice(h * RET_DK, (h + 1) * RET_DK)
        q = q_ref[:, sl]
        k = k_ref[:, sl]
        qr = jnp.where(valid, (q * cosf + pltpu.roll(q, RET_DK // 2, 1) * sins) * scale, 0.0)
        kr = jnp.where(valid, k * cosf + pltpu.roll(k, RET_DK // 2, 1) * sins, 0.0)
        v = jnp.where(valid, v_ref[:, sl], 0.0)
        qb = qr.astype(BF16)
        kb = kr.astype(BF16)
        vb = v.astype(BF16)
        sc = lax.dot_general(qb, kb, NT, preferred_element_type=F32) * dmask_ref[h]
        inner = jnp.dot(sc.astype(BF16), vb, preferred_element_type=F32)
        s_old = s_scr[h]
        cross = jnp.dot(qb, s_old.astype(BF16), preferred_element_type=F32) * cdec_ref[h]
        o = inner + cross
        kd_t = (kr * kdec_ref[h]).T.astype(BF16)
        gdec = jnp.where(is_last, g_last[h], g_full[h])
        s_scr[h] = gdec * s_old + jnp.dot(kd_t, vb, preferred_element_type=F32)
        g = g_ref[:, sl]
        y_ref[:, sl] = (_ln(o) * (g * jax.nn.sigmoid(g))).astype(BF16)

    @pl.when(is_last)
    def _():
        s_out_ref[...] = s_scr[...]


def _ret_prompt(rq, rk, rv, rg, cosf, sins):
    b, t, _ = rq.shape
    nc = pl.cdiv(t, LANES)
    last_rows = t - (nc - 1) * LANES
    dmask, cdec, kdec, g_full, g_last = _ret_tables(last_rows)
    blk = pl.BlockSpec((None, LANES, 512), lambda bi, ci: (bi, ci, 0))
    tab = pl.BlockSpec((LANES, RET_DK), lambda bi, ci: (ci, 0))
    full3 = pl.BlockSpec((RET_HEADS, LANES, LANES), lambda bi, ci: (0, 0, 0))
    return pl.pallas_call(
        functools.partial(_ret_prompt_kernel, t_len=t, g_full=g_full, g_last=g_last),
        grid=(b, nc),
        in_specs=[blk, blk, blk, blk, tab, tab, full3, full3,
                  pl.BlockSpec((None, RET_HEADS, LANES, LANES), lambda bi, ci: (ci // (nc - 1), 0, 0, 0))],
        out_specs=[blk, pl.BlockSpec((None, RET_HEADS, RET_DK, RET_DK), lambda bi, ci: (bi, 0, 0, 0))],
        out_shape=[jax.ShapeDtypeStruct((b, t, 512), BF16),
                   jax.ShapeDtypeStruct((b, RET_HEADS, RET_DK, RET_DK), F32)],
        scratch_shapes=[pltpu.VMEM((RET_HEADS, RET_DK, RET_DK), F32)],
        compiler_params=_params(("parallel", "arbitrary")),
        name="ret_prompt",
    )(rq, rk, rv, rg, cosf, sins, dmask, cdec, kdec)


def _ret_sample_kernel(q_ref, k_ref, qt_ref, kt_ref, v_ref, g_ref, cos_ref, sin_ref, cost_ref, sint_ref,
                       s0_ref, y_ref, s1_ref, o_scr, *, gammas):
    h = pl.program_id(0)
    gamma = jnp.float32(gammas[0])
    for i in range(1, RET_HEADS):
        gamma = jnp.where(h == i, jnp.float32(gammas[i]), gamma)
    scale = RET_DK ** -0.5
    half = RET_DK // 2

    def rot_t(x):
        return x * cost_ref[...] + jnp.concatenate([x[half:], x[:half]], axis=0) * sint_ref[...]

    def rot(x):
        return x * cos_ref[...] + pltpu.roll(x, half, 1) * sin_ref[...]

    def r16(x):
        return x.astype(BF16).astype(F32)

    qt = r16(rot_t(qt_ref[...]) * scale)
    kt = r16(rot_t(kt_ref[...]))
    q = r16(rot(q_ref[...]) * scale)
    k = r16(rot(k_ref[...]))
    vb = r16(v_ref[...])
    qk = r16(jnp.sum(q * k, axis=1, keepdims=True))
    inner = qk * vb
    for b in range(q.shape[0]):
        s_old = s0_ref[b]
        qc = qt[:, b:b + 1]
        kc = kt[:, b:b + 1]
        o_scr[b:b + 1, :] = jnp.sum(qc * r16(s_old), axis=0, keepdims=True) * gamma
        s1_ref[b] = gamma * s_old + kc * vb[b:b + 1, :]
    o = inner + o_scr[...]
    g = g_ref[...]
    y_ref[...] = (_ln(o) * (g * jax.nn.sigmoid(g))).astype(BF16)


def _ret_sample(rq, rk, rv, rg, state, pos):
    nb = rq.shape[0]
    cosf, sins = _rot_tables(jnp.full((1,), pos, I32))
    cos_b = jnp.broadcast_to(cosf, (nb, RET_DK))
    sin_b = jnp.broadcast_to(sins, (nb, RET_DK))
    cos_t = jnp.broadcast_to(cosf.reshape(RET_DK, 1), (RET_DK, nb))
    sin_t = jnp.broadcast_to(sins.reshape(RET_DK, 1), (RET_DK, nb))
    qt = rq.reshape(nb, RET_HEADS, RET_DK).transpose(1, 2, 0)
    kt = rk.reshape(nb, RET_HEADS, RET_DK).transpose(1, 2, 0)
    col = pl.BlockSpec((nb, RET_DK), lambda h: (0, h))
    tr = pl.BlockSpec((None, RET_DK, nb), lambda h: (h, 0, 0))
    full = pl.BlockSpec((nb, RET_DK), lambda h: (0, 0))
    full_t = pl.BlockSpec((RET_DK, nb), lambda h: (0, 0))
    st = pl.BlockSpec((nb, None, RET_DK, RET_DK), lambda h: (0, h, 0, 0))
    return pl.pallas_call(
        functools.partial(_ret_sample_kernel, gammas=_ret_gammas()),
        grid=(RET_HEADS,),
        in_specs=[col, col, tr, tr, col, col, full, full, full_t, full_t, st],
        out_specs=[col, st],
        out_shape=[jax.ShapeDtypeStruct((nb, 512), BF16), jax.ShapeDtypeStruct(state.shape, F32)],
        scratch_shapes=[pltpu.VMEM((nb, RET_DK), F32)],
        compiler_params=_params(("parallel",)),
        name="ret_sample",
    )(rq, rk, qt, kt, rv, rg, cos_b, sin_b, cos_t, sin_t, state)


def _bucket_np(d):
    d = np.maximum(d, 0)
    lb = REL_MAX_EXACT + (np.log(np.maximum(d, 1).astype(np.float32) / np.float32(REL_MAX_EXACT))
                          / np.float32(math.log(REL_MAX_DIST / REL_MAX_EXACT))
                          * np.float32(REL_BUCKETS - REL_MAX_EXACT)).astype(np.int32)
    return np.where(d < REL_MAX_EXACT, d, np.minimum(lb, REL_BUCKETS - 1)).astype(np.int32)


def _bias_kernel(rb_ref, bidx_ref, bidx_s_ref, out_ref, out_s_ref):
    for m in range(2):
        bi = bidx_ref[m]
        for h in range(DSA_HEADS):
            acc = jnp.zeros((LANES, LANES), F32)
            for b in range(REL_BUCKETS):
                acc = jnp.where(bi == b, rb_ref[b, h], acc)
            out_ref[h, m] = acc
    for m in range(3):
        bi = bidx_s_ref[m:m + 1, :]
        for h in range(DSA_HEADS):
            acc = jnp.zeros((1, LANES), F32)
            for b in range(REL_BUCKETS):
                acc = jnp.where(bi == b, rb_ref[b, h], acc)
            out_s_ref[m, h:h + 1, :] = acc


def _bias_tables(rel_bias):
    r = np.arange(LANES)
    d = r[:, None] - r[None, :]
    bidx = jnp.asarray(np.stack([_bucket_np(d), _bucket_np(d + LANES)]), I32)
    assert int(_bucket_np(np.array([LANES]))[0]) == REL_BUCKETS - 1
    bidx_s = jnp.asarray(np.stack([np.full(LANES, REL_BUCKETS - 1), _bucket_np(LANES - r), np.zeros(LANES)]), I32)
    return pl.pallas_call(
        _bias_kernel,
        in_specs=[pl.BlockSpec(memory_space=pltpu.SMEM), pl.BlockSpec(memory_space=pltpu.VMEM),
                  pl.BlockSpec(memory_space=pltpu.VMEM)],
        out_specs=[pl.BlockSpec(memory_space=pltpu.VMEM), pl.BlockSpec(memory_space=pltpu.VMEM)],
        out_shape=[jax.ShapeDtypeStruct((DSA_HEADS, 2, LANES, LANES), F32),
                   jax.ShapeDtypeStruct((3, DSA_HEADS, LANES), F32)],
        name="bias_tables",
    )(rel_bias, bidx, bidx_s)


def _order_key(x):
    bits = pltpu.bitcast(jnp.where(x == 0.0, 0.0, x), I32)
    return jnp.where(bits < 0, bits ^ jnp.int32(0x7FFFFFFF), bits)


def _count(mask):
    return jnp.sum(jnp.where(mask, 1.0, 0.0), axis=1, keepdims=True)


def _select_mask(key_ref, j0_ref, n_sel, col_bits, row_ok=None):
    rows, width = key_ref.shape

    def search(it, res):
        cand = res | lax.shift_left(jnp.int32(1), 31 - it)
        cnt = _count(key_ref[...] >= (cand ^ INT_MIN))
        return jnp.where(cnt >= n_sel, cand, res)

    thr = lax.fori_loop(0, 32, search, jnp.zeros((rows, 1), I32), unroll=8) ^ INT_MIN
    key = key_ref[...]
    need = n_sel - _count(key > thr)
    n_ge = jnp.where(thr == INT_MIN, 0.0, _count(key >= thr))
    if row_ok is not None:
        n_ge = jnp.where(row_ok, n_ge, 0.0)
    j0_ref[...] = jnp.full((rows, 1), width, I32)

    @pl.when(jnp.max(n_ge) > n_sel)
    def _():
        def tie_search(it, res):
            cand = res | lax.shift_left(jnp.int32(1), col_bits - 1 - it)
            col = lax.broadcasted_iota(I32, (rows, width), 1)
            before = jnp.sum(jnp.where(key_ref[...] == thr, jnp.where(col < cand, 1.0, 0.0), 0.0),
                             axis=1, keepdims=True)
            return jnp.where(before < need, cand, res)

        j0_ref[...] = lax.fori_loop(0, col_bits, tie_search, jnp.zeros((rows, 1), I32))

    col = lax.broadcasted_iota(I32, (rows, width), 1)
    ninf = jnp.float32(-jnp.inf)
    tie_ok = jnp.where(key == thr, jnp.where(col <= j0_ref[...], 0.0, ninf), ninf)
    return jnp.where(key > thr, 0.0, tie_ok)


def _dsa_prompt_tile(i, tkw, bfar_ref, dq_ref, iq_ref, tq_ref, bt_ref, o_ref,
                     kpad, vpad, kipad, key_scr, am_scr, lg_scr, j0_scr, *, t_len, n_sel, col_bits):
    key_v = key_scr.at[:, 0:tkw]
    am_v = am_scr.at[:, 0:tkw]
    lg_v = lg_scr.at[:, 0:tkw]

    rowpos = i * LANES + lax.broadcasted_iota(I32, (LANES, 1), 0)
    row_ok = rowpos < t_len

    qi = jnp.where(row_ok, iq_ref[...], 0.0).astype(BF16)
    w = jnp.where(row_ok, tq_ref[:, IDX_DIM:IDX_DIM + DSA_HEADS], 0.0)
    ki = kipad[0:tkw, 0:IDX_DIM]
    sc = None
    for h in range(DSA_HEADS):
        d = lax.dot_general(qi[:, h * IDX_DIM:(h + 1) * IDX_DIM], ki, NT, preferred_element_type=F32)
        t = w[:, h:h + 1] * jnp.maximum(d, 0.0)
        sc = t if sc is None else sc + t

    col = lax.broadcasted_iota(I32, (LANES, tkw), 1)
    vis = col <= rowpos
    key_v[...] = jnp.where(vis, _order_key(sc), INT_MIN)
    am_v[...] = jnp.where(vis, _select_mask(key_v, j0_scr, n_sel, col_bits, row_ok), -jnp.inf)

    q = jnp.where(row_ok, dq_ref[...], 0.0) * (DSA_DH ** -0.5)
    c0 = pl.multiple_of(i * LANES, LANES)
    c1 = pl.multiple_of(jnp.maximum(i - 1, 0) * LANES, LANES)
    for h in range(DSA_HEADS):
        hs = slice(h * DSA_DH, (h + 1) * DSA_DH)
        qh = q[:, hs].astype(BF16)
        lg_v[...] = (lax.dot_general(qh, kpad[0:tkw, hs], NT, preferred_element_type=F32)
                     + bfar_ref[h] + am_v[...])
        for m, cm in ((1, c1), (0, c0)):
            blk = lax.dot_general(qh, kpad[pl.ds(cm, LANES), hs], NT, preferred_element_type=F32)
            lg_scr[:, pl.ds(cm, LANES)] = blk + bt_ref[h, m] + am_scr[:, pl.ds(cm, LANES)]
        lg = lg_v[...]
        mx = jnp.max(lg, axis=1, keepdims=True)
        p = jnp.exp(lg - mx)
        den = jnp.sum(p, axis=1, keepdims=True)
        oh = jnp.dot(p.astype(BF16), vpad[0:tkw, hs], preferred_element_type=F32) / den
        o_ref[:, hs] = oh.astype(BF16)


def _dsa_prompt_kernel(bfar_ref, dq_ref, iq_ref, tq_ref, dk_ref, dv_ref, tk_ref, bt_ref, o_ref,
                       kpad, vpad, kipad, key_scr, am_scr, lg_scr, j0_scr, *, t_len, tk_pad, widths, n_sel, col_bits):
    i = pl.program_id(1)

    @pl.when(i == 0)
    def _():
        kpad[0:t_len] = dk_ref[...].astype(BF16)
        kpad[t_len:tk_pad] = jnp.zeros((tk_pad - t_len, DSA_W), BF16)
        vpad[0:t_len] = dv_ref[...].astype(BF16)
        vpad[t_len:tk_pad] = jnp.zeros((tk_pad - t_len, DSA_W), BF16)
        kipad[0:t_len] = tk_ref[...].astype(BF16)
        kipad[t_len:tk_pad] = jnp.zeros((tk_pad - t_len, PROJ_TAIL), BF16)

    lo = 0
    for nblk in widths:
        @pl.when(jnp.logical_and(i >= lo, i < nblk))
        def _(nblk=nblk):
            _dsa_prompt_tile(i, nblk * LANES, bfar_ref, dq_ref, iq_ref, tq_ref, bt_ref, o_ref,
                             kpad, vpad, kipad, key_scr, am_scr, lg_scr, j0_scr,
                             t_len=t_len, n_sel=n_sel, col_bits=col_bits)
        lo = nblk


CAUSAL_GROUPS = 8


def _dsa_prompt(dq, dk, dv, iq, tail, bias_tiles, bias_far):
    b, t, _ = dq.shape
    n_sel = min(TOPK_MAX, t // 4)
    nq = pl.cdiv(t, LANES)
    tk_pad = nq * LANES
    col_bits = max(1, int(math.ceil(math.log2(tk_pad + 1))))
    widths = sorted({int(math.ceil(nq * j / CAUSAL_GROUPS)) for j in range(1, CAUSAL_GROUPS + 1)})
    qblk = lambda wd: pl.BlockSpec((None, LANES, wd), lambda bi, qi: (bi, qi, 0))
    kblk = lambda wd: pl.BlockSpec((None, t, wd), lambda bi, qi: (bi, 0, 0))
    return pl.pallas_call(
        functools.partial(_dsa_prompt_kernel, t_len=t, tk_pad=tk_pad, widths=widths, n_sel=n_sel, col_bits=col_bits),
        grid=(b, nq),
        in_specs=[pl.BlockSpec(memory_space=pltpu.SMEM), qblk(DSA_W), qblk(DSA_W), qblk(PROJ_TAIL),
                  kblk(DSA_W), kblk(DSA_W), kblk(PROJ_TAIL),
                  pl.BlockSpec((DSA_HEADS, 2, LANES, LANES), lambda bi, qi: (0, 0, 0, 0))],
        out_specs=qblk(DSA_W),
        out_shape=jax.ShapeDtypeStruct((b, t, DSA_W), BF16),
        scratch_shapes=[pltpu.VMEM((tk_pad, DSA_W), BF16), pltpu.VMEM((tk_pad, DSA_W), BF16),
                        pltpu.VMEM((tk_pad, PROJ_TAIL), BF16), pltpu.VMEM((LANES, tk_pad), I32),
                        pltpu.VMEM((LANES, tk_pad), F32), pltpu.VMEM((LANES, tk_pad), F32),
                        pltpu.VMEM((LANES, 1), I32)],
        compiler_params=_params(("parallel", "arbitrary")),
        name="dsa_prompt",
    )(bias_far, dq, iq, tail, dk, dv, tail, bias_tiles)


def _idx_scores_kernel(pt_ref, qi_ref, w_ref, kin_ref, kidx_hbm, out_ref, buf, sem, *, n_pages, page, past):
    b = pl.program_id(0)
    slot = b % 2

    def page_copy(sl, p, src_page):
        dst = buf.at[sl, :, pl.ds(pl.multiple_of(p * page, page), page)]
        return pltpu.make_async_copy(kidx_hbm.at[src_page], dst, sem.at[sl])

    def start_all(bb, sl):
        def start(p, c):
            page_copy(sl, p, pt_ref[bb, p]).start()
            return c

        lax.fori_loop(0, n_pages, start, 0)

    @pl.when(b == 0)
    def _():
        start_all(b, slot)

    @pl.when(b + 1 < pl.num_programs(0))
    def _():
        start_all(b + 1, 1 - slot)

    def wait(p, c):
        page_copy(slot, p, 0).wait()
        return c

    lax.fori_loop(0, n_pages, wait, 0)
    qi = qi_ref[...].astype(BF16)
    w = w_ref[...]
    d = jnp.dot(qi, buf[slot].astype(BF16), preferred_element_type=F32)
    out_ref[:, 0:past] = jnp.sum(w * jnp.maximum(d, 0.0), axis=0, keepdims=True)
    kn = kin_ref[...].astype(BF16).astype(F32)
    dn = jnp.sum(qi.astype(F32) * kn, axis=1, keepdims=True)
    sn = jnp.sum(w * jnp.maximum(dn, 0.0), axis=0, keepdims=True)
    lane = lax.broadcasted_iota(I32, (1, page), 1)
    out_ref[:, past:past + page] = jnp.where(lane == 0, sn, -jnp.inf)


def _idx_scores(page_table, qi, w, ki_new, kidx_t):
    nb, n_pages = page_table.shape
    page = kidx_t.shape[2]
    past = n_pages * page
    width = past + page
    grid_spec = pltpu.PrefetchScalarGridSpec(
        num_scalar_prefetch=1,
        grid=(nb,),
        in_specs=[pl.BlockSpec((None, DSA_HEADS, IDX_DIM), lambda b, pt: (b, 0, 0)),
                  pl.BlockSpec((None, DSA_HEADS, 1), lambda b, pt: (b, 0, 0)),
                  pl.BlockSpec((None, 1, IDX_DIM), lambda b, pt: (b, 0, 0)),
                  pl.BlockSpec(memory_space=pl.ANY)],
        out_specs=pl.BlockSpec((None, 1, width), lambda b, pt: (b, 0, 0)),
        scratch_shapes=[pltpu.VMEM((2, IDX_DIM, past), F32), pltpu.SemaphoreType.DMA((2,))],
    )
    return pl.pallas_call(
        functools.partial(_idx_scores_kernel, n_pages=n_pages, page=page, past=past),
        grid_spec=grid_spec,
        out_shape=jax.ShapeDtypeStruct((nb, 1, width), F32),
        compiler_params=_params(("arbitrary",)),
        name="idx_scores",
    )(page_table, qi, w, ki_new, kidx_t)


def _sel_mask_kernel(s_ref, am_ref, key_scr, j0_scr, *, n_sel, col_bits):
    key_scr[...] = _order_key(s_ref[...])
    am_ref[...] = _select_mask(key_scr, j0_scr, n_sel, col_bits)


def _sel_mask(scores, n_sel):
    nb, width = scores.shape
    col_bits = max(1, int(math.ceil(math.log2(width + 1))))
    return pl.pallas_call(
        functools.partial(_sel_mask_kernel, n_sel=n_sel, col_bits=col_bits),
        in_specs=[pl.BlockSpec(memory_space=pltpu.VMEM)],
        out_specs=pl.BlockSpec(memory_space=pltpu.VMEM),
        out_shape=jax.ShapeDtypeStruct((nb, width), F32),
        scratch_shapes=[pltpu.VMEM((nb, width), I32), pltpu.VMEM((nb, 1), I32)],
        compiler_params=pltpu.CompilerParams(vmem_limit_bytes=VMEM_LIMIT),
        name="sel_mask",
    )(scores)


PAGES_PER_STEP = 16
PAGE_SLOTS = 3


def _hrows(h):
    return slice(h * DSA_DH, (h + 1) * DSA_DH)


def _eye_dh():
    return lax.broadcasted_iota(I32, (DSA_DH, DSA_DH), 0) == lax.broadcasted_iota(I32, (DSA_DH, DSA_DH), 1)


MXU_ROWS = 16


def _sample_begin(q_ref, qb_scr, m_scr, l_scr, acc_scr):
    q = q_ref[...] * (DSA_DH ** -0.5)
    head_of_col = lax.broadcasted_iota(I32, (MXU_ROWS, DSA_W), 1) // DSA_DH
    row = lax.broadcasted_iota(I32, (MXU_ROWS, DSA_W), 0)
    qb_scr[...] = jnp.where(head_of_col == row, jnp.broadcast_to(q, (MXU_ROWS, DSA_W)), 0.0)
    m_scr[...] = jnp.full(m_scr.shape, -1e30, F32)
    l_scr[...] = jnp.zeros_like(l_scr)
    acc_scr[...] = jnp.zeros_like(acc_scr)


def _sample_chunk(kchunk, vchunk, am_row, bs_ref, last_chunk, qb_scr, m_scr, l_scr, acc_scr):
    pps = kchunk.shape[0]
    qb = qb_scr[...].astype(BF16)
    lgs = []
    for i in range(pps):
        lg_i = jnp.dot(qb, kchunk[i].reshape(DSA_W, LANES).astype(BF16), preferred_element_type=F32)[0:DSA_HEADS]
        lgs.append(lg_i + (bs_ref[1] if last_chunk and i == pps - 1 else bs_ref[0]))
    lg = jnp.concatenate(lgs, axis=1) + am_row
    m_old = m_scr[:, 0:1]
    m_new = jnp.maximum(m_old, jnp.max(lg, axis=1, keepdims=True))
    alpha = jnp.exp(m_old - m_new)
    p = jnp.exp(lg - m_new)
    l_scr[...] = jnp.broadcast_to(l_scr[:, 0:1] * alpha + jnp.sum(p, axis=1, keepdims=True), l_scr.shape)
    m_scr[...] = jnp.broadcast_to(m_new, m_scr.shape)
    vcat = jnp.concatenate([vchunk[i].reshape(DSA_W, LANES).astype(BF16) for i in range(pps)], axis=1)
    p16 = jnp.concatenate([p, jnp.zeros((MXU_ROWS - DSA_HEADS, p.shape[1]), F32)], axis=0).astype(BF16)
    pv = lax.dot_general(vcat, p16, NT, preferred_element_type=F32)
    alpha_rows = jnp.concatenate([jnp.broadcast_to(alpha[h:h + 1, :], (DSA_DH, 1)) for h in range(DSA_HEADS)], axis=0)
    acc_scr[...] = acc_scr[...] * alpha_rows + pv


def _sample_finish(q_ref, kn_ref, vn_ref, am_new, bs_ref, o_ref, m_scr, l_scr, acc_scr):
    q = q_ref[...] * (DSA_DH ** -0.5)
    kn = kn_ref[...]
    vn = vn_ref[...]
    outs = []
    for h in range(DSA_HEADS):
        lgn = (jnp.sum(q[:, _hrows(h)] * kn[:, _hrows(h)], axis=1, keepdims=True)
               + bs_ref[2][h:h + 1, 0:1] + am_new)
        mo = m_scr[h:h + 1, 0:1]
        mn = jnp.maximum(mo, lgn)
        al = jnp.exp(mo - mn)
        pn = jnp.exp(lgn - mn)
        den = l_scr[h:h + 1, 0:1] * al + pn
        ocol = acc_scr[_hrows(h), h:h + 1]
        orow = jnp.sum(jnp.where(_eye_dh(), jnp.broadcast_to(ocol, (DSA_DH, DSA_DH)), 0.0), axis=0, keepdims=True)
        outs.append((orow * al + pn * vn[:, _hrows(h)]) / den)
    o_ref[...] = jnp.concatenate(outs, axis=1).astype(BF16)


def _route_sample_kernel(pt_sm, st_ref, q_ref, kn_ref, vn_ref, am_ref, bs_ref, ck_hbm, cv_hbm,
                         g_ref, e_ref, o_ref, kbuf, vbuf, sem, qb_scr, m_scr, l_scr, acc_scr, *, nb, n_chunks):
    pps = PAGES_PER_STEP
    t = pl.program_id(0)
    n_steps = pl.num_programs(0)
    heads_per_chunk = PEER_HEADS // n_chunks
    past = n_chunks * pps * LANES

    n_slots = kbuf.shape[0]
    ahead = n_slots - 1

    def slot_of(step, c):
        return (step * n_chunks + c) % n_slots

    def chunk_copies(step, c, lookup):
        sl = slot_of(step, c)
        cps = []
        for i in range(pps):
            pg = pt_sm[jnp.minimum(step, nb - 1), c * pps + i] if lookup else 0
            cps.append(pltpu.make_async_copy(ck_hbm.at[pg], kbuf.at[sl, i], sem.at[0, sl]))
            cps.append(pltpu.make_async_copy(cv_hbm.at[pg], vbuf.at[sl, i], sem.at[1, sl]))
        return cps

    @pl.when(t == 0)
    def _():
        for c in range(ahead):
            for cp in chunk_copies(t, c, True):
                cp.start()

    _sample_begin(q_ref, qb_scr, m_scr, l_scr, acc_scr)
    for c in range(n_chunks):
        if c + ahead < n_chunks:
            for cp in chunk_copies(t, c + ahead, True):
                cp.start()
        else:
            @pl.when(t + 1 < n_steps)
            def _(c=c):
                for cp in chunk_copies(t + 1, c + ahead - n_chunks, True):
                    cp.start()
        for cp in chunk_copies(t, c, False):
            cp.wait()
        sl = slot_of(t, c)
        _sample_chunk(kbuf.at[sl], vbuf.at[sl], am_ref[:, c * pps * LANES:(c + 1) * pps * LANES], bs_ref,
                      c == n_chunks - 1, qb_scr, m_scr, l_scr, acc_scr)
        for hh in range(c * heads_per_chunk, (c + 1) * heads_per_chunk):
            _route_head(st_ref, g_ref, e_ref, hh)
    _sample_finish(q_ref, kn_ref, vn_ref, am_ref[:, past:past + 1], bs_ref, o_ref, m_scr, l_scr, acc_scr)


def _route_sample(st, page_table, dq, dk_new, dv_new, am, bias_rows, ck_t, cv_t):
    n = st.shape[2]
    n_tiles = n // LANES
    nb, n_pages = page_table.shape
    page = ck_t.shape[3]
    pps = PAGES_PER_STEP
    n_chunks = n_pages // pps
    assert PEER_HEADS % n_chunks == 0 and PAGE_SLOTS - 1 <= n_chunks
    n_steps = max(n_tiles, nb)
    tile = lambda t, pt: jnp.minimum(t, n_tiles - 1)
    row = lambda wd: pl.BlockSpec((None, 1, wd), lambda t, pt: (jnp.minimum(t, nb - 1), 0, 0))
    blk = pl.BlockSpec((HK, LANES), lambda t, pt: (0, tile(t, pt)))
    grid_spec = pltpu.PrefetchScalarGridSpec(
        num_scalar_prefetch=1,
        grid=(n_steps,),
        in_specs=[pl.BlockSpec((2 * PEER_HEADS, PEER_NKEYS, LANES), lambda t, pt: (0, 0, tile(t, pt))),
                  row(DSA_W), row(DSA_W), row(DSA_W), row(am.shape[-1]),
                  pl.BlockSpec((3, DSA_HEADS, LANES), lambda t, pt: (0, 0, 0)),
                  pl.BlockSpec(memory_space=pl.ANY), pl.BlockSpec(memory_space=pl.ANY)],
        out_specs=[blk, blk, row(DSA_W)],
        scratch_shapes=[pltpu.VMEM((PAGE_SLOTS, pps, DSA_HEADS, DSA_DH, page), F32),
                        pltpu.VMEM((PAGE_SLOTS, pps, DSA_HEADS, DSA_DH, page), F32),
                        pltpu.SemaphoreType.DMA((2, PAGE_SLOTS)),
                        pltpu.VMEM((MXU_ROWS, DSA_W), F32), pltpu.VMEM((DSA_HEADS, LANES), F32),
                        pltpu.VMEM((DSA_HEADS, LANES), F32), pltpu.VMEM((DSA_W, MXU_ROWS), F32)],
    )
    r3 = lambda a: a.reshape(nb, 1, a.shape[-1])
    return pl.pallas_call(
        functools.partial(_route_sample_kernel, nb=nb, n_chunks=n_chunks),
        grid_spec=grid_spec,
        out_shape=[jax.ShapeDtypeStruct((HK, n), F32), jax.ShapeDtypeStruct((HK, n), I32),
                   jax.ShapeDtypeStruct((nb, 1, DSA_W), BF16)],
        compiler_params=_params(("arbitrary",)),
        name="route_sample",
    )(page_table, st, r3(dq), r3(dk_new), r3(dv_new), r3(am), bias_rows, ck_t, cv_t)


def _tail1_kernel(ry_ref, do_ref, h_ref, wo_ref, g1_ref, b1_ref, wq_ref, sk_ref, h1_ref, h1b_ref, st_ref):
    half = wo_ref.shape[0] // 2
    mix = (jnp.dot(ry_ref[...], wo_ref[0:half], preferred_element_type=F32)
           + jnp.dot(do_ref[...], wo_ref[half:], preferred_element_type=F32))
    h1 = _ln(DN_ALPHA * h_ref[...] + mix) * g1_ref[...] + b1_ref[...]
    h1_ref[...] = h1
    h1b = h1.astype(BF16)
    h1b_ref[...] = h1b
    q = jnp.dot(h1b, wq_ref[...], preferred_element_type=F32)
    for hh in range(PEER_HEADS):
        qh = _ln(q[:, hh * PEER_DKEY:(hh + 1) * PEER_DKEY]).astype(BF16)
        for s in range(2):
            qs = qh[:, s * (PEER_DKEY // 2):(s + 1) * (PEER_DKEY // 2)]
            st_ref[hh * 2 + s] = lax.dot_general(sk_ref[hh, s], qs, NT, preferred_element_type=F32)


def _tail1(ret_y, dsa_o, h, wo, g1, b1, wq, sk, tm):
    n = h.shape[0]
    row = lambda wd: pl.BlockSpec((tm, wd), lambda i: (i, 0))
    fixed = lambda shp: pl.BlockSpec(shp, lambda i: (0,) * len(shp))
    return pl.pallas_call(
        _tail1_kernel,
        grid=(n // tm,),
        in_specs=[row(512), row(512), row(D_MODEL), fixed(wo.shape), fixed((1, D_MODEL)), fixed((1, D_MODEL)),
                  fixed(wq.shape), fixed(sk.shape)],
        out_specs=[row(D_MODEL), row(D_MODEL),
                   pl.BlockSpec((2 * PEER_HEADS, PEER_NKEYS, tm), lambda i: (0, 0, i))],
        out_shape=[jax.ShapeDtypeStruct((n, D_MODEL), F32), jax.ShapeDtypeStruct((n, D_MODEL), BF16),
                   jax.ShapeDtypeStruct((2 * PEER_HEADS, PEER_NKEYS, n), F32)],
        compiler_params=_params(("parallel",)),
        name="tail1",
    )(ret_y, dsa_o, h, wo, g1, b1, wq, sk)


def _top16(s):
    kio = lax.broadcasted_iota(I32, s.shape, 0).astype(F32)
    vals, idxs = [], []
    for _ in range(PEER_TOPK):
        m = jnp.max(s, axis=0, keepdims=True)
        first = jnp.min(jnp.where(s == m, kio, float(s.shape[0])), axis=0, keepdims=True)
        s = jnp.where(kio == first, -jnp.inf, s)
        vals.append(m)
        idxs.append(first)
    return jnp.concatenate(vals, 0), jnp.concatenate(idxs, 0)


def _route_head(st_ref, g_ref, e_ref, hh):
    r8 = lax.broadcasted_iota(I32, (8, LANES), 0).astype(F32)
    v1, i1 = _top16(st_ref[2 * hh])
    v2, i2 = _top16(st_ref[2 * hh + 1])
    vals = [v1[0:8] + v2[0:1], v1[8:16] + v2[0:1]]
    flat = [r8 * 16.0, (r8 + 8.0) * 16.0]
    exp_id = [i1[0:8] * PEER_NKEYS + i2[0:1], i1[8:16] * PEER_NKEYS + i2[0:1]]
    for j in range(1, 8):
        vals.append(v1[0:8] + v2[j:j + 1])
        flat.append(r8 * 16.0 + float(j))
        exp_id.append(i1[0:8] * PEER_NKEYS + i2[j:j + 1])
    vals.append(v1[0:1] + v2[8:16])
    flat.append(r8 + 8.0)
    exp_id.append(i1[0:1] * PEER_NKEYS + i2[8:16])
    cand = jnp.concatenate(vals, 0)
    pos = jnp.concatenate(flat, 0)
    eid = jnp.concatenate(exp_id, 0)
    tops, es = [], []
    for _ in range(PEER_TOPK):
        m = jnp.max(cand, axis=0, keepdims=True)
        pm = jnp.min(jnp.where(cand == m, pos, 1e9), axis=0, keepdims=True)
        hit = pos == pm
        es.append(jnp.max(jnp.where(hit, eid, -1.0), axis=0, keepdims=True))
        cand = jnp.where(hit, -jnp.inf, cand)
        tops.append(m)
    top = jnp.concatenate(tops, 0)
    ex = jnp.exp(top - top[0:1])
    off = hh * PEER_TOPK if isinstance(hh, int) else pl.multiple_of(hh * PEER_TOPK, PEER_TOPK)
    g_ref[pl.ds(off, PEER_TOPK), :] = ex / jnp.sum(ex, axis=0, keepdims=True)
    e_ref[pl.ds(off, PEER_TOPK), :] = jnp.concatenate(es, 0).astype(I32)


def _route_kernel(st_ref, g_ref, e_ref):
    def head(hh, c):
        _route_head(st_ref, g_ref, e_ref, hh)
        return c

    lax.fori_loop(0, PEER_HEADS, head, 0, unroll=2)


def _route(st):
    n = st.shape[2]
    blk = pl.BlockSpec((HK, LANES), lambda i: (0, i))
    return pl.pallas_call(
        _route_kernel,
        grid=(n // LANES,),
        in_specs=[pl.BlockSpec((2 * PEER_HEADS, PEER_NKEYS, LANES), lambda i: (0, 0, i))],
        out_specs=[blk, blk],
        out_shape=[jax.ShapeDtypeStruct((HK, n), F32), jax.ShapeDtypeStruct((HK, n), I32)],
        compiler_params=_params(("parallel",)),
        name="route",
    )(st)


EXPERT_BLOCK = 1024
EXPERT_SUB = 256


def _peer_act_kernel(x_ref, u_ref, e_ref, g_ref, w_ref, acc_scr):
    j = pl.program_id(1)

    @pl.when(j == 0)
    def _():
        acc_scr[...] = jnp.zeros_like(acc_scr)

    x = x_ref[...]
    e = e_ref[...]
    i2 = e & (PEER_NKEYS - 1)
    i1 = e >> 7
    acc = acc_scr[...]
    for c in range(EXPERT_BLOCK // EXPERT_SUB):
        hmat = lax.dot_general(x, u_ref[c * EXPERT_SUB:(c + 1) * EXPERT_SUB, :].astype(BF16), NT,
                               preferred_element_type=F32)
        for s in range(EXPERT_SUB // PEER_NKEYS):
            got = jnp.take_along_axis(hmat[:, s * PEER_NKEYS:(s + 1) * PEER_NKEYS], i2, axis=1,
                                      mode="promise_in_bounds")
            row1 = (j * EXPERT_BLOCK + c * EXPERT_SUB) // PEER_NKEYS + s
            acc = jnp.where(i1 == row1, got, acc)
    acc_scr[...] = acc

    @pl.when(j == pl.num_programs(1) - 1)
    def _():
        a = acc_scr[...]
        gelu = 0.5 * a * (1.0 + lax.erf(a * (2.0 ** -0.5)))
        w_ref[...] = g_ref[...] * gelu


def _peer_act(xb, u, e, g, tr):
    n = xb.shape[0]
    row = lambda wd: pl.BlockSpec((tr, wd), lambda i, j: (i, 0))
    return pl.pallas_call(
        _peer_act_kernel,
        grid=(n // tr, u.shape[0] // EXPERT_BLOCK),
        in_specs=[row(D_MODEL), pl.BlockSpec((EXPERT_BLOCK, D_MODEL), lambda i, j: (j, 0)), row(HK), row(HK)],
        out_specs=row(HK),
        out_shape=jax.ShapeDtypeStruct((n, HK), F32),
        scratch_shapes=[pltpu.VMEM((tr, HK), F32)],
        compiler_params=_params(("parallel", "arbitrary")),
        name="peer_act",
    )(xb, u, e, g)


GATE_GROUP = 8


def _peer_out_kernel(e_ref, w_ref, v_ref, h1_ref, g2_ref, b2_ref, y_ref, p_scr, acc_scr):
    k = pl.program_id(1)
    tr = e_ref.shape[0]
    rows_per_step = v_ref.shape[0] // PEER_NKEYS
    sub8 = p_scr.shape[1]

    @pl.when(k == 0)
    def _():
        acc_scr[...] = jnp.zeros_like(acc_scr)
        sub = lax.broadcasted_iota(I32, (PEER_NKEYS, HK), 0)

        def body(grp, c):
            r0 = pl.multiple_of(grp * GATE_GROUP, GATE_GROUP)
            ers = e_ref[pl.ds(r0, GATE_GROUP), :]
            wrs = w_ref[pl.ds(r0, GATE_GROUP), :]
            ps = []
            for t in range(GATE_GROUP):
                er = ers[t:t + 1, :]
                wr = wrs[t:t + 1, :]
                o1 = jnp.where(sub == (er >> 7), 1.0, 0.0).astype(BF16)
                o2 = jnp.where(sub == (er & (PEER_NKEYS - 1)), wr, 0.0).astype(BF16)
                ps.append(lax.dot_general(o1, o2, NT, preferred_element_type=F32))
            x = jnp.stack(ps).reshape(GATE_GROUP, PEER_NKEYS // sub8, sub8, PEER_NKEYS)
            p_scr[:, :, pl.ds(r0, GATE_GROUP), :] = pltpu.einshape("tqsl->qstl", x)
            return c

        lax.fori_loop(0, tr // GATE_GROUP, body, 0, unroll=4)

    tot = None
    for j in range(0, rows_per_step, 2):
        q = (k * rows_per_step + j) // sub8
        lhs = jnp.concatenate([p_scr[q, j % sub8], p_scr[q, (j + 1) % sub8]], axis=1).astype(BF16)
        d = jnp.dot(lhs, v_ref[j * PEER_NKEYS:(j + 2) * PEER_NKEYS, :], preferred_element_type=F32)
        tot = d if tot is None else tot + d
    acc_scr[...] += tot

    @pl.when(k == pl.num_programs(1) - 1)
    def _():
        y_ref[...] = _ln(DN_ALPHA * h1_ref[...] + acc_scr[...]) * g2_ref[...] + b2_ref[...]


def _peer_out(e, w, v, h1, g2, b2, tr, kc):
    n = h1.shape[0]
    row = lambda wd: pl.BlockSpec((tr, wd), lambda i, k: (i, 0))
    vec = pl.BlockSpec((1, D_MODEL), lambda i, k: (0, 0))
    return pl.pallas_call(
        _peer_out_kernel,
        grid=(n // tr, v.shape[0] // kc),
        in_specs=[row(HK), row(HK), pl.BlockSpec((kc, D_MODEL), lambda i, k: (k, 0)), row(D_MODEL), vec, vec],
        out_specs=row(D_MODEL),
        out_shape=jax.ShapeDtypeStruct((n, D_MODEL), F32),
        scratch_shapes=[pltpu.VMEM((PEER_NKEYS // GATE_GROUP, GATE_GROUP, tr, PEER_NKEYS), F32),
                        pltpu.VMEM((tr, D_MODEL), F32)],
        compiler_params=_params(("parallel", "arbitrary")),
        name="peer_out",
    )(e, w, v, h1, g2, b2)


def _peer_experts(g_t, e_t, h1, h1b, wts, tr_act, tr_out, kc):
    _, _, _, g2, b2, _, _, u, v = wts
    g = g_t.T
    e = e_t.T
    w = _peer_act(h1b, u, e, g, tr_act)
    return _peer_out(e, w, v, h1, g2, b2, tr_out, kc)


def kernel(x_prompt, x_sample, cache_k, cache_v, cache_kidx, state_ret, page_table, meta_tokens, rel_bias,
           w_in, w_out, ln1_g, ln1_b, ln2_g, ln2_b, peer_wq, peer_subkeys, peer_u, peer_v):
    assert w_in.shape[0] == 1 and x_sample.shape[1] == 1
    nbp, seq, _ = x_prompt.shape
    t = seq + N_META
    nbs = x_sample.shape[0]
    n_pages = page_table.shape[1]
    page = cache_kidx.shape[2]
    past = n_pages * page
    assert page == LANES and n_pages % PAGES_PER_STEP == 0

    hp = jnp.concatenate([jnp.broadcast_to(meta_tokens[None], (nbp, N_META, D_MODEL)), x_prompt], 1)
    hp = hp.reshape(nbp * t, D_MODEL)
    hs = x_sample.reshape(nbs, D_MODEL)

    w = w_in[0]
    wa = w[:, :PROJ_MAIN].astype(BF16)
    wt = jnp.pad(w[:, PROJ_MAIN:], ((0, 0), (0, PROJ_TAIL - (w.shape[1] - PROJ_MAIN)))).astype(BF16)
    wts = (w_out[0].astype(BF16), ln1_g, ln1_b, ln2_g, ln2_b, peer_wq[0].astype(BF16),
           peer_subkeys[0].astype(BF16), peer_u[0], peer_v[0].astype(BF16))
    bias_tiles, bias_rows = _bias_tables(rel_bias)

    rq, rk, rv, rg, dq, dk, dv, iq, tail = _inproj(hp, wa, wt, 384)
    b3 = lambda a: a.reshape(nbp, t, a.shape[-1])
    cosf, sins = _rot_tables(jnp.arange(t, dtype=I32))
    ret_y, ret_s = _ret_prompt(b3(rq), b3(rk), b3(rv), b3(rg), cosf, sins)
    dsa_o = _dsa_prompt(b3(dq), b3(dk), b3(dv), b3(iq), b3(tail), bias_tiles, rel_bias[REL_BUCKETS - 1])
    wo, g1, b1, _, _, wq, sk, _, _ = wts
    h1p, h1bp, stp = _tail1(ret_y.reshape(nbp * t, 512), dsa_o.reshape(nbp * t, DSA_W), hp, wo, g1, b1, wq, sk, 384)

    kidx_t = jnp.transpose(cache_kidx[0], (0, 2, 1))
    ck_t = jnp.transpose(cache_k[0], (0, 2, 3, 1))
    cv_t = jnp.transpose(cache_v[0], (0, 2, 3, 1))
    srq, srk, srv, srg, sdq, sdk, sdv, siq, stail = _inproj(hs, wa, wt, nbs)
    sret_y, sret_s = _ret_sample(srq, srk, srv, srg, state_ret[0], past)
    scores = _idx_scores(page_table, siq.reshape(nbs, DSA_HEADS, IDX_DIM),
                         stail[:, IDX_DIM:IDX_DIM + DSA_HEADS].reshape(nbs, DSA_HEADS, 1),
                         stail[:, :IDX_DIM].reshape(nbs, 1, IDX_DIM), kidx_t)
    n_sel = min(TOPK_MAX, (past + 1) // 4)
    am = _sel_mask(scores.reshape(nbs, past + page), n_sel)
    g_tp, e_tp, sdsa_o = _route_sample(stp, page_table, sdq, sdk, sdv, am, bias_rows, ck_t, cv_t)
    yp = _peer_experts(g_tp, e_tp, h1p, h1bp, wts, 1376, 384, 4096)
    y_prompt = yp.reshape(nbp, t, D_MODEL)[:, N_META:]
    h1s, h1bs, sts = _tail1(sret_y, sdsa_o.reshape(nbs, DSA_W), hs, wo, g1, b1, wq, sk, nbs)
    g_ts, e_ts = _route(sts)
    ys = _peer_experts(g_ts, e_ts, h1s, h1bs, wts, nbs, nbs, 2048)

    kv_p = lambda a: a.reshape(1, nbp, t, DSA_HEADS, DSA_DH)
    kv_s = lambda a: a.reshape(1, nbs, 1, DSA_HEADS, DSA_DH)
    return (y_prompt, ys.reshape(nbs, 1, D_MODEL), kv_p(dk), kv_p(dv),
            tail[:, :IDX_DIM].reshape(1, nbp, t, IDX_DIM), ret_s[None],
            kv_s(sdk), kv_s(sdv), stail[:, :IDX_DIM].reshape(1, nbs, 1, IDX_DIM), sret_s[None])
```

```python
import functools
import math

import numpy as np
import jax
import jax.numpy as jnp
from jax import lax
from jax.experimental import pallas as pl
from jax.experimental.pallas import tpu as pltpu

F32 = jnp.float32
BF16 = jnp.bfloat16
I32 = jnp.int32

D_MODEL = 1024
N_META = 16
RET_HEADS = 4
RET_DK = 128
DSA_HEADS = 8
DSA_DH = 64
DSA_W = DSA_HEADS * DSA_DH
IDX_DIM = 64
TOPK_MAX = 256
REL_BUCKETS = 32
REL_MAX_EXACT = 16
REL_MAX_DIST = 128
PEER_HEADS = 8
PEER_NKEYS = 128
PEER_DKEY = 256
PEER_TOPK = 16
HK = PEER_HEADS * PEER_TOPK
DN_ALPHA = 2.0 ** 0.25
LN_EPS = 1e-5
PROJ_MAIN = 4096
PROJ_TAIL = 128

LANES = 128
INT_MIN = -2 ** 31
VMEM_LIMIT = 56 * 1024 * 1024

NT = (((1,), (1,)), ((), ()))


def _params(sem, vmem=VMEM_LIMIT):
    return pltpu.CompilerParams(dimension_semantics=sem, vmem_limit_bytes=vmem)


def _ln(x):
    mu = jnp.mean(x, axis=-1, keepdims=True)
    xc = x - mu
    var = jnp.mean(xc * xc, axis=-1, keepdims=True)
    return xc * lax.rsqrt(var + LN_EPS)


def _inproj_kernel(x_ref, wa_ref, wt_ref, *out_refs):
    x = x_ref[...].astype(BF16)
    for j in range(8):
        out_refs[j][...] = jnp.dot(x, wa_ref[:, j * 512:(j + 1) * 512], preferred_element_type=F32)
    out_refs[8][...] = jnp.dot(x, wt_ref[...], preferred_element_type=F32)


def _inproj(x, wa, wt, tm):
    n = x.shape[0]
    row = lambda i: (i, 0)
    fixed = lambda i: (0, 0)
    return pl.pallas_call(
        _inproj_kernel,
        grid=(n // tm,),
        in_specs=[pl.BlockSpec((tm, D_MODEL), row), pl.BlockSpec((D_MODEL, PROJ_MAIN), fixed),
                  pl.BlockSpec((D_MODEL, PROJ_TAIL), fixed)],
        out_specs=[pl.BlockSpec((tm, 512), row)] * 8 + [pl.BlockSpec((tm, PROJ_TAIL), row)],
        out_shape=[jax.ShapeDtypeStruct((n, 512), F32)] * 8 + [jax.ShapeDtypeStruct((n, PROJ_TAIL), F32)],
        compiler_params=_params(("parallel",)),
        name="inproj",
    )(x, wa, wt)


def _ret_gammas():
    return [float(np.exp(np.log(np.float32(1.0 - 2.0 ** (-5.0 - h))))) for h in range(RET_HEADS)]


def _ret_tables(last_rows):
    c = LANES
    lg = np.log(1.0 - 2.0 ** (-5.0 - np.arange(RET_HEADS, dtype=np.float64)))
    i = np.arange(c, dtype=np.float64)
    diff = i[:, None] - i[None, :]
    dmask = np.where(diff[None] >= 0, np.exp(np.maximum(diff[None], 0.0) * lg[:, None, None]), 0.0)
    cdec = np.exp((i[None, :] + 1.0) * lg[:, None])[:, :, None] * np.ones((1, 1, c))
    kfull = np.exp((c - 1.0 - i)[None, :] * lg[:, None])
    klast = np.where(i[None, :] < last_rows, np.exp(np.maximum(last_rows - 1.0 - i, 0.0)[None, :] * lg[:, None]), 0.0)
    kdec = np.stack([kfull, klast])[:, :, :, None] * np.ones((1, 1, 1, c))
    g_full = [float(np.exp(c * l)) for l in lg]
    g_last = [float(np.exp(last_rows * l)) for l in lg]
    return (jnp.asarray(dmask, F32), jnp.asarray(cdec, F32), jnp.asarray(kdec, F32), g_full, g_last)


def _rot_tables(pos):
    half = RET_DK // 2
    inv = 1.0 / (10000.0 ** (jnp.arange(half, dtype=F32) / half))
    ang = pos.astype(F32)[:, None] * inv[None, :]
    cos, sin = jnp.cos(ang), jnp.sin(ang)
    return jnp.concatenate([cos, cos], -1), jnp.concatenate([-sin, sin], -1)


def _ret_prompt_kernel(q_ref, k_ref, v_ref, g_ref, cos_ref, sin_ref, dmask_ref, cdec_ref, kdec_ref,
                       y_ref, s_out_ref, s_scr, *, t_len, g_full, g_last):
    c = pl.program_id(1)
    is_last = c == pl.num_programs(1) - 1

    @pl.when(c == 0)
    def _():
        s_scr[...] = jnp.zeros_like(s_scr)

    row = lax.broadcasted_iota(I32, (LANES, 1), 0) + c * LANES
    valid = row < t_len
    cosf = cos_ref[...]
    sins = sin_ref[...]
    scale = RET_DK ** -0.5
    for h in range(RET_HEADS):
        sl = slice(h * RET_DK, (h + 1) * RET_DK)
        q = q_ref[:, sl]
        k = k_ref[:, sl]
        qr = jnp.where(valid, (q * cosf + pltpu.roll(q, RET_DK // 2, 1) * sins) * scale, 0.0)
        kr = jnp.where(valid, k * cosf + pltpu.roll(k, RET_DK // 2, 1) * sins, 0.0)
        v = jnp.where(valid, v_ref[:, sl], 0.0)
        qb = qr.astype(BF16)
        kb = kr.astype(BF16)
        vb = v.astype(BF16)
        sc = lax.dot_general(qb, kb, NT, preferred_element_type=F32) * dmask_ref[h]
        inner = jnp.dot(sc.astype(BF16), vb, preferred_element_type=F32)
        s_old = s_scr[h]
        cross = jnp.dot(qb, s_old.astype(BF16), preferred_element_type=F32) * cdec_ref[h]
        o = inner + cross
        kd_t = (kr * kdec_ref[h]).T.astype(BF16)
        gdec = jnp.where(is_last, g_last[h], g_full[h])
        s_scr[h] = gdec * s_old + jnp.dot(kd_t, vb, preferred_element_type=F32)
        g = g_ref[:, sl]
        y_ref[:, sl] = (_ln(o) * (g * jax.nn.sigmoid(g))).astype(BF16)

    @pl.when(is_last)
    def _():
        s_out_ref[...] = s_scr[...]


def _ret_prompt(rq, rk, rv, rg, cosf, sins):
    b, t, _ = rq.shape
    nc = pl.cdiv(t, LANES)
    last_rows = t - (nc - 1) * LANES
    dmask, cdec, kdec, g_full, g_last = _ret_tables(last_rows)
    blk = pl.BlockSpec((None, LANES, 512), lambda bi, ci: (bi, ci, 0))
    tab = pl.BlockSpec((LANES, RET_DK), lambda bi, ci: (ci, 0))
    full3 = pl.BlockSpec((RET_HEADS, LANES, LANES), lambda bi, ci: (0, 0, 0))
    return pl.pallas_call(
        functools.partial(_ret_prompt_kernel, t_len=t, g_full=g_full, g_last=g_last),
        grid=(b, nc),
        in_specs=[blk, blk, blk, blk, tab, tab, full3, full3,
                  pl.BlockSpec((None, RET_HEADS, LANES, LANES), lambda bi, ci: (ci // (nc - 1), 0, 0, 0))],
        out_specs=[blk, pl.BlockSpec((None, RET_HEADS, RET_DK, RET_DK), lambda bi, ci: (bi, 0, 0, 0))],
        out_shape=[jax.ShapeDtypeStruct((b, t, 512), BF16),
                   jax.ShapeDtypeStruct((b, RET_HEADS, RET_DK, RET_DK), F32)],
        scratch_shapes=[pltpu.VMEM((RET_HEADS, RET_DK, RET_DK), F32)],
        compiler_params=_params(("parallel", "arbitrary")),
        name="ret_prompt",
    )(rq, rk, rv, rg, cosf, sins, dmask, cdec, kdec)


def _ret_sample_kernel(q_ref, k_ref, qt_ref, kt_ref, v_ref, g_ref, cos_ref, sin_ref, cost_ref, sint_ref,
                       s0_ref, y_ref, s1_ref, o_scr, *, gammas):
    h = pl.program_id(0)
    gamma = jnp.float32(gammas[0])
    for i in range(1, RET_HEADS):
        gamma = jnp.where(h == i, jnp.float32(gammas[i]), gamma)
    scale = RET_DK ** -0.5
    half = RET_DK // 2

    def rot_t(x):
        return x * cost_ref[...] + jnp.concatenate([x[half:], x[:half]], axis=0) * sint_ref[...]

    def rot(x):
        return x * cos_ref[...] + pltpu.roll(x, half, 1) * sin_ref[...]

    def r16(x):
        return x.astype(BF16).astype(F32)

    qt = r16(rot_t(qt_ref[...]) * scale)
    kt = r16(rot_t(kt_ref[...]))
    q = r16(rot(q_ref[...]) * scale)
    k = r16(rot(k_ref[...]))
    vb = r16(v_ref[...])
    qk = r16(jnp.sum(q * k, axis=1, keepdims=True))
    inner = qk * vb
    for b in range(q.shape[0]):
        s_old = s0_ref[b]
        qc = qt[:, b:b + 1]
        kc = kt[:, b:b + 1]
        o_scr[b:b + 1, :] = jnp.sum(qc * r16(s_old), axis=0, keepdims=True) * gamma
        s1_ref[b] = gamma * s_old + kc * vb[b:b + 1, :]
    o = inner + o_scr[...]
    g = g_ref[...]
    y_ref[...] = (_ln(o) * (g * jax.nn.sigmoid(g))).astype(BF16)


def _ret_sample(rq, rk, rv, rg, state, pos):
    nb = rq.shape[0]
    cosf, sins = _rot_tables(jnp.full((1,), pos, I32))
    cos_b = jnp.broadcast_to(cosf, (nb, RET_DK))
    sin_b = jnp.broadcast_to(sins, (nb, RET_DK))
    cos_t = jnp.broadcast_to(cosf.reshape(RET_DK, 1), (RET_DK, nb))
    sin_t = jnp.broadcast_to(sins.reshape(RET_DK, 1), (RET_DK, nb))
    qt = rq.reshape(nb, RET_HEADS, RET_DK).transpose(1, 2, 0)
    kt = rk.reshape(nb, RET_HEADS, RET_DK).transpose(1, 2, 0)
    col = pl.BlockSpec((nb, RET_DK), lambda h: (0, h))
    tr = pl.BlockSpec((None, RET_DK, nb), lambda h: (h, 0, 0))
    full = pl.BlockSpec((nb, RET_DK), lambda h: (0, 0))
    full_t = pl.BlockSpec((RET_DK, nb), lambda h: (0, 0))
    st = pl.BlockSpec((nb, None, RET_DK, RET_DK), lambda h: (0, h, 0, 0))
    return pl.pallas_call(
        functools.partial(_ret_sample_kernel, gammas=_ret_gammas()),
        grid=(RET_HEADS,),
        in_specs=[col, col, tr, tr, col, col, full, full, full_t, full_t, st],
        out_specs=[col, st],
        out_shape=[jax.ShapeDtypeStruct((nb, 512), BF16), jax.ShapeDtypeStruct(state.shape, F32)],
        scratch_shapes=[pltpu.VMEM((nb, RET_DK), F32)],
        compiler_params=_params(("parallel",)),
        name="ret_sample",
    )(rq, rk, qt, kt, rv, rg, cos_b, sin_b, cos_t, sin_t, state)


def _bucket_np(d):
    d = np.maximum(d, 0)
    lb = REL_MAX_EXACT + (np.log(np.maximum(d, 1).astype(np.float32) / np.float32(REL_MAX_EXACT))
                          / np.float32(math.log(REL_MAX_DIST / REL_MAX_EXACT))
                          * np.float32(REL_BUCKETS - REL_MAX_EXACT)).astype(np.int32)
    return np.where(d < REL_MAX_EXACT, d, np.minimum(lb, REL_BUCKETS - 1)).astype(np.int32)


def _bias_kernel(rb_ref, bidx_ref, bidx_s_ref, out_ref, out_s_ref):
    for m in range(2):
        bi = bidx_ref[m]
        for h in range(DSA_HEADS):
            acc = jnp.zeros((LANES, LANES), F32)
            for b in range(REL_BUCKETS):
                acc = jnp.where(bi == b, rb_ref[b, h], acc)
            out_ref[h, m] = acc
    for m in range(3):
        bi = bidx_s_ref[m:m + 1, :]
        for h in range(DSA_HEADS):
            acc = jnp.zeros((1, LANES), F32)
            for b in range(REL_BUCKETS):
                acc = jnp.where(bi == b, rb_ref[b, h], acc)
            out_s_ref[m, h:h + 1, :] = acc


def _bias_tables(rel_bias):
    r = np.arange(LANES)
    d = r[:, None] - r[None, :]
    bidx = jnp.asarray(np.stack([_bucket_np(d), _bucket_np(d + LANES)]), I32)
    assert int(_bucket_np(np.array([LANES]))[0]) == REL_BUCKETS - 1
    bidx_s = jnp.asarray(np.stack([np.full(LANES, REL_BUCKETS - 1), _bucket_np(LANES - r), np.zeros(LANES)]), I32)
    return pl.pallas_call(
        _bias_kernel,
        in_specs=[pl.BlockSpec(memory_space=pltpu.SMEM), pl.BlockSpec(memory_space=pltpu.VMEM),
                  pl.BlockSpec(memory_space=pltpu.VMEM)],
        out_specs=[pl.BlockSpec(memory_space=pltpu.VMEM), pl.BlockSpec(memory_space=pltpu.VMEM)],
        out_shape=[jax.ShapeDtypeStruct((DSA_HEADS, 2, LANES, LANES), F32),
                   jax.ShapeDtypeStruct((3, DSA_HEADS, LANES), F32)],
        name="bias_tables",
    )(rel_bias, bidx, bidx_s)


def _order_key(x):
    bits = pltpu.bitcast(jnp.where(x == 0.0, 0.0, x), I32)
    return jnp.where(bits < 0, bits ^ jnp.int32(0x7FFFFFFF), bits)


def _count(mask):
    return jnp.sum(jnp.where(mask, 1.0, 0.0), axis=1, keepdims=True)


def _select_mask(key_ref, j0_ref, n_sel, col_bits, row_ok=None):
    rows, width = key_ref.shape

    def search(it, res):
        cand = res | lax.shift_left(jnp.int32(1), 31 - it)
        cnt = _count(key_ref[...] >= (cand ^ INT_MIN))
        return jnp.where(cnt >= n_sel, cand, res)

    thr = lax.fori_loop(0, 32, search, jnp.zeros((rows, 1), I32), unroll=8) ^ INT_MIN
    key = key_ref[...]
    need = n_sel - _count(key > thr)
    n_ge = jnp.where(thr == INT_MIN, 0.0, _count(key >= thr))
    if row_ok is not None:
        n_ge = jnp.where(row_ok, n_ge, 0.0)
    j0_ref[...] = jnp.full((rows, 1), width, I32)

    @pl.when(jnp.max(n_ge) > n_sel)
    def _():
        def tie_search(it, res):
            cand = res | lax.shift_left(jnp.int32(1), col_bits - 1 - it)
            col = lax.broadcasted_iota(I32, (rows, width), 1)
            before = jnp.sum(jnp.where(key_ref[...] == thr, jnp.where(col < cand, 1.0, 0.0), 0.0),
                             axis=1, keepdims=True)
            return jnp.where(before < need, cand, res)

        j0_ref[...] = lax.fori_loop(0, col_bits, tie_search, jnp.zeros((rows, 1), I32))

    col = lax.broadcasted_iota(I32, (rows, width), 1)
    ninf = jnp.float32(-jnp.inf)
    tie_ok = jnp.where(key == thr, jnp.where(col <= j0_ref[...], 0.0, ninf), ninf)
    return jnp.where(key > thr, 0.0, tie_ok)


def _dsa_prompt_tile(i, tkw, bfar_ref, dq_ref, iq_ref, tq_ref, bt_ref, o_ref,
                     kpad, vpad, kipad, key_scr, am_scr, lg_scr, j0_scr, *, t_len, n_sel, col_bits):
    key_v = key_scr.at[:, 0:tkw]
    am_v = am_scr.at[:, 0:tkw]
    lg_v = lg_scr.at[:, 0:tkw]

    rowpos = i * LANES + lax.broadcasted_iota(I32, (LANES, 1), 0)
    row_ok = rowpos < t_len

    qi = jnp.where(row_ok, iq_ref[...], 0.0).astype(BF16)
    w = jnp.where(row_ok, tq_ref[:, IDX_DIM:IDX_DIM + DSA_HEADS], 0.0)
    ki = kipad[0:tkw, 0:IDX_DIM]
    sc = None
    for h in range(DSA_HEADS):
        d = lax.dot_general(qi[:, h * IDX_DIM:(h + 1) * IDX_DIM], ki, NT, preferred_element_type=F32)
        t = w[:, h:h + 1] * jnp.maximum(d, 0.0)
        sc = t if sc is None else sc + t

    col = lax.broadcasted_iota(I32, (LANES, tkw), 1)
    vis = col <= rowpos
    key_v[...] = jnp.where(vis, _order_key(sc), INT_MIN)
    am_v[...] = jnp.where(vis, _select_mask(key_v, j0_scr, n_sel, col_bits, row_ok), -jnp.inf)

    q = jnp.where(row_ok, dq_ref[...], 0.0) * (DSA_DH ** -0.5)
    c0 = pl.multiple_of(i * LANES, LANES)
    c1 = pl.multiple_of(jnp.maximum(i - 1, 0) * LANES, LANES)
    for h in range(DSA_HEADS):
        hs = slice(h * DSA_DH, (h + 1) * DSA_DH)
        qh = q[:, hs].astype(BF16)
        lg_v[...] = (lax.dot_general(qh, kpad[0:tkw, hs], NT, preferred_element_type=F32)
                     + bfar_ref[h] + am_v[...])
        for m, cm in ((1, c1), (0, c0)):
            blk = lax.dot_general(qh, kpad[pl.ds(cm, LANES), hs], NT, preferred_element_type=F32)
            lg_scr[:, pl.ds(cm, LANES)] = blk + bt_ref[h, m] + am_scr[:, pl.ds(cm, LANES)]
        lg = lg_v[...]
        mx = jnp.max(lg, axis=1, keepdims=True)
        p = jnp.exp(lg - mx)
        den = jnp.sum(p, axis=1, keepdims=True)
        oh = jnp.dot(p.astype(BF16), vpad[0:tkw, hs], preferred_element_type=F32) / den
        o_ref[:, hs] = oh.astype(BF16)


def _dsa_prompt_kernel(bfar_ref, dq_ref, iq_ref, tq_ref, dk_ref, dv_ref, tk_ref, bt_ref, o_ref,
                       kpad, vpad, kipad, key_scr, am_scr, lg_scr, j0_scr, *, t_len, tk_pad, widths, n_sel, col_bits):
    i = pl.program_id(1)

    @pl.when(i == 0)
    def _():
        kpad[0:t_len] = dk_ref[...].astype(BF16)
        kpad[t_len:tk_pad] = jnp.zeros((tk_pad - t_len, DSA_W), BF16)
        vpad[0:t_len] = dv_ref[...].astype(BF16)
        vpad[t_len:tk_pad] = jnp.zeros((tk_pad - t_len, DSA_W), BF16)
        kipad[0:t_len] = tk_ref[...].astype(BF16)
        kipad[t_len:tk_pad] = jnp.zeros((tk_pad - t_len, PROJ_TAIL), BF16)

    lo = 0
    for nblk in widths:
        @pl.when(jnp.logical_and(i >= lo, i < nblk))
        def _(nblk=nblk):
            _dsa_prompt_tile(i, nblk * LANES, bfar_ref, dq_ref, iq_ref, tq_ref, bt_ref, o_ref,
                             kpad, vpad, kipad, key_scr, am_scr, lg_scr, j0_scr,
                             t_len=t_len, n_sel=n_sel, col_bits=col_bits)
        lo = nblk


def _dsa_prompt(dq, dk, dv, iq, tail, bias_tiles, bias_far):
    b, t, _ = dq.shape
    n_sel = min(TOPK_MAX, t // 4)
    nq = pl.cdiv(t, LANES)
    tk_pad = nq * LANES
    col_bits = max(1, int(math.ceil(math.log2(tk_pad + 1))))
    widths = sorted({int(math.ceil(nq * j / 4)) for j in range(1, 5)})
    qblk = lambda wd: pl.BlockSpec((None, LANES, wd), lambda bi, qi: (bi, qi, 0))
    kblk = lambda wd: pl.BlockSpec((None, t, wd), lambda bi, qi: (bi, 0, 0))
    return pl.pallas_call(
        functools.partial(_dsa_prompt_kernel, t_len=t, tk_pad=tk_pad, widths=widths, n_sel=n_sel, col_bits=col_bits),
        grid=(b, nq),
        in_specs=[pl.BlockSpec(memory_space=pltpu.SMEM), qblk(DSA_W), qblk(DSA_W), qblk(PROJ_TAIL),
                  kblk(DSA_W), kblk(DSA_W), kblk(PROJ_TAIL),
                  pl.BlockSpec((DSA_HEADS, 2, LANES, LANES), lambda bi, qi: (0, 0, 0, 0))],
        out_specs=qblk(DSA_W),
        out_shape=jax.ShapeDtypeStruct((b, t, DSA_W), BF16),
        scratch_shapes=[pltpu.VMEM((tk_pad, DSA_W), BF16), pltpu.VMEM((tk_pad, DSA_W), BF16),
                        pltpu.VMEM((tk_pad, PROJ_TAIL), BF16), pltpu.VMEM((LANES, tk_pad), I32),
                        pltpu.VMEM((LANES, tk_pad), F32), pltpu.VMEM((LANES, tk_pad), F32),
                        pltpu.VMEM((LANES, 1), I32)],
        compiler_params=_params(("parallel", "arbitrary")),
        name="dsa_prompt",
    )(bias_far, dq, iq, tail, dk, dv, tail, bias_tiles)


def _idx_scores_kernel(pt_ref, qi_ref, w_ref, kin_ref, kidx_hbm, out_ref, buf, sem, *, n_pages, page, past):
    b = pl.program_id(0)
    slot = b % 2

    def page_copy(sl, p, src_page):
        dst = buf.at[sl, :, pl.ds(pl.multiple_of(p * page, page), page)]
        return pltpu.make_async_copy(kidx_hbm.at[src_page], dst, sem.at[sl])

    def start_all(bb, sl):
        def start(pair, c):
            for pri in range(2):
                p = 2 * pair + pri
                page_copy(sl, p, pt_ref[bb, p]).start(priority=pri)
            return c

        lax.fori_loop(0, n_pages // 2, start, 0)

    @pl.when(b == 0)
    def _():
        start_all(b, slot)

    @pl.when(b + 1 < pl.num_programs(0))
    def _():
        start_all(b + 1, 1 - slot)

    def wait(p, c):
        page_copy(slot, p, 0).wait()
        return c

    lax.fori_loop(0, n_pages, wait, 0)
    qi = qi_ref[...].astype(BF16)
    w = w_ref[...]
    d = jnp.dot(qi, buf[slot].astype(BF16), preferred_element_type=F32)
    out_ref[:, 0:past] = jnp.sum(w * jnp.maximum(d, 0.0), axis=0, keepdims=True)
    kn = kin_ref[...].astype(BF16).astype(F32)
    dn = jnp.sum(qi.astype(F32) * kn, axis=1, keepdims=True)
    sn = jnp.sum(w * jnp.maximum(dn, 0.0), axis=0, keepdims=True)
    lane = lax.broadcasted_iota(I32, (1, page), 1)
    out_ref[:, past:past + page] = jnp.where(lane == 0, sn, -jnp.inf)


def _idx_scores(page_table, qi, w, ki_new, kidx_t):
    nb, n_pages = page_table.shape
    page = kidx_t.shape[2]
    past = n_pages * page
    width = past + page
    grid_spec = pltpu.PrefetchScalarGridSpec(
        num_scalar_prefetch=1,
        grid=(nb,),
        in_specs=[pl.BlockSpec((None, DSA_HEADS, IDX_DIM), lambda b, pt: (b, 0, 0)),
                  pl.BlockSpec((None, DSA_HEADS, 1), lambda b, pt: (b, 0, 0)),
                  pl.BlockSpec((None, 1, IDX_DIM), lambda b, pt: (b, 0, 0)),
                  pl.BlockSpec(memory_space=pl.ANY)],
        out_specs=pl.BlockSpec((None, 1, width), lambda b, pt: (b, 0, 0)),
        scratch_shapes=[pltpu.VMEM((2, IDX_DIM, past), F32), pltpu.SemaphoreType.DMA((2,))],
    )
    return pl.pallas_call(
        functools.partial(_idx_scores_kernel, n_pages=n_pages, page=page, past=past),
        grid_spec=grid_spec,
        out_shape=jax.ShapeDtypeStruct((nb, 1, width), F32),
        compiler_params=_params(("arbitrary",)),
        name="idx_scores",
    )(page_table, qi, w, ki_new, kidx_t)


def _sel_mask_kernel(s_ref, am_ref, key_scr, j0_scr, *, n_sel, col_bits):
    key_scr[...] = _order_key(s_ref[...])
    am_ref[...] = _select_mask(key_scr, j0_scr, n_sel, col_bits)


def _sel_mask(scores, n_sel):
    nb, width = scores.shape
    col_bits = max(1, int(math.ceil(math.log2(width + 1))))
    return pl.pallas_call(
        functools.partial(_sel_mask_kernel, n_sel=n_sel, col_bits=col_bits),
        in_specs=[pl.BlockSpec(memory_space=pltpu.VMEM)],
        out_specs=pl.BlockSpec(memory_space=pltpu.VMEM),
        out_shape=jax.ShapeDtypeStruct((nb, width), F32),
        scratch_shapes=[pltpu.VMEM((nb, width), I32), pltpu.VMEM((nb, 1), I32)],
        compiler_params=pltpu.CompilerParams(vmem_limit_bytes=VMEM_LIMIT),
        name="sel_mask",
    )(scores)


PAGES_PER_STEP = 16
PAGE_SLOTS = 3


def _hrows(h):
    return slice(h * DSA_DH, (h + 1) * DSA_DH)


def _eye_dh():
    return lax.broadcasted_iota(I32, (DSA_DH, DSA_DH), 0) == lax.broadcasted_iota(I32, (DSA_DH, DSA_DH), 1)


MXU_ROWS = 16


def _sample_begin(q_ref, qb_scr, m_scr, l_scr, acc_scr):
    q = q_ref[...] * (DSA_DH ** -0.5)
    head_of_col = lax.broadcasted_iota(I32, (MXU_ROWS, DSA_W), 1) // DSA_DH
    row = lax.broadcasted_iota(I32, (MXU_ROWS, DSA_W), 0)
    qb_scr[...] = jnp.where(head_of_col == row, jnp.broadcast_to(q, (MXU_ROWS, DSA_W)), 0.0)
    m_scr[...] = jnp.full(m_scr.shape, -1e30, F32)
    l_scr[...] = jnp.zeros_like(l_scr)
    acc_scr[...] = jnp.zeros_like(acc_scr)


def _sample_chunk(kchunk, vchunk, am_row, bs_ref, last_chunk, qb_scr, m_scr, l_scr, acc_scr):
    pps = kchunk.shape[0]
    qb = qb_scr[...].astype(BF16)
    lgs = []
    for i in range(pps):
        lg_i = jnp.dot(qb, kchunk[i].reshape(DSA_W, LANES).astype(BF16), preferred_element_type=F32)[0:DSA_HEADS]
        lgs.append(lg_i + (bs_ref[1] if last_chunk and i == pps - 1 else bs_ref[0]))
    lg = jnp.concatenate(lgs, axis=1) + am_row
    m_old = m_scr[:, 0:1]
    m_new = jnp.maximum(m_old, jnp.max(lg, axis=1, keepdims=True))
    alpha = jnp.exp(m_old - m_new)
    p = jnp.exp(lg - m_new)
    l_scr[...] = jnp.broadcast_to(l_scr[:, 0:1] * alpha + jnp.sum(p, axis=1, keepdims=True), l_scr.shape)
    m_scr[...] = jnp.broadcast_to(m_new, m_scr.shape)
    vcat = jnp.concatenate([vchunk[i].reshape(DSA_W, LANES).astype(BF16) for i in range(pps)], axis=1)
    p16 = jnp.concatenate([p, jnp.zeros((MXU_ROWS - DSA_HEADS, p.shape[1]), F32)], axis=0).astype(BF16)
    pv = lax.dot_general(vcat, p16, NT, preferred_element_type=F32)
    alpha_rows = jnp.concatenate([jnp.broadcast_to(alpha[h:h + 1, :], (DSA_DH, 1)) for h in range(DSA_HEADS)], axis=0)
    acc_scr[...] = acc_scr[...] * alpha_rows + pv


def _sample_finish(q_ref, kn_ref, vn_ref, am_new, bs_ref, o_ref, m_scr, l_scr, acc_scr):
    q = q_ref[...] * (DSA_DH ** -0.5)
    kn = kn_ref[...]
    vn = vn_ref[...]
    outs = []
    for h in range(DSA_HEADS):
        lgn = (jnp.sum(q[:, _hrows(h)] * kn[:, _hrows(h)], axis=1, keepdims=True)
               + bs_ref[2][h:h + 1, 0:1] + am_new)
        mo = m_scr[h:h + 1, 0:1]
        mn = jnp.maximum(mo, lgn)
        al = jnp.exp(mo - mn)
        pn = jnp.exp(lgn - mn)
        den = l_scr[h:h + 1, 0:1] * al + pn
        ocol = acc_scr[_hrows(h), h:h + 1]
        orow = jnp.sum(jnp.where(_eye_dh(), jnp.broadcast_to(ocol, (DSA_DH, DSA_DH)), 0.0), axis=0, keepdims=True)
        outs.append((orow * al + pn * vn[:, _hrows(h)]) / den)
    o_ref[...] = jnp.concatenate(outs, axis=1).astype(BF16)


def _route_sample_kernel(pt_sm, st_ref, q_ref, kn_ref, vn_ref, am_ref, bs_ref, ck_hbm, cv_hbm,
                         g_ref, e_ref, o_ref, kbuf, vbuf, sem, qb_scr, m_scr, l_scr, acc_scr, *, nb, n_chunks):
    pps = PAGES_PER_STEP
    t = pl.program_id(0)
    n_steps = pl.num_programs(0)
    heads_per_chunk = PEER_HEADS // n_chunks
    past = n_chunks * pps * LANES

    n_slots = kbuf.shape[0]
    ahead = n_slots - 1

    def slot_of(step, c):
        return (step * n_chunks + c) % n_slots

    def chunk_copies(step, c, lookup):
        sl = slot_of(step, c)
        cps = []
        for i in range(pps):
            pg = pt_sm[jnp.minimum(step, nb - 1), c * pps + i] if lookup else 0
            cps.append(pltpu.make_async_copy(ck_hbm.at[pg], kbuf.at[sl, i], sem.at[0, sl]))
            cps.append(pltpu.make_async_copy(cv_hbm.at[pg], vbuf.at[sl, i], sem.at[1, sl]))
        return cps

    @pl.when(t == 0)
    def _():
        for c in range(ahead):
            for cp in chunk_copies(t, c, True):
                cp.start()

    _sample_begin(q_ref, qb_scr, m_scr, l_scr, acc_scr)
    for c in range(n_chunks):
        if c + ahead < n_chunks:
            for cp in chunk_copies(t, c + ahead, True):
                cp.start()
        else:
            @pl.when(t + 1 < n_steps)
            def _(c=c):
                for cp in chunk_copies(t + 1, c + ahead - n_chunks, True):
                    cp.start()
        for cp in chunk_copies(t, c, False):
            cp.wait()
        sl = slot_of(t, c)
        _sample_chunk(kbuf.at[sl], vbuf.at[sl], am_ref[:, c * pps * LANES:(c + 1) * pps * LANES], bs_ref,
                      c == n_chunks - 1, qb_scr, m_scr, l_scr, acc_scr)
        for hh in range(c * heads_per_chunk, (c + 1) * heads_per_chunk):
            _route_head(st_ref, g_ref, e_ref, hh)
    _sample_finish(q_ref, kn_ref, vn_ref, am_ref[:, past:past + 1], bs_ref, o_ref, m_scr, l_scr, acc_scr)


def _route_sample(st, page_table, dq, dk_new, dv_new, am, bias_rows, ck_t, cv_t):
    n = st.shape[2]
    n_tiles = n // LANES
    nb, n_pages = page_table.shape
    page = ck_t.shape[3]
    pps = PAGES_PER_STEP
    n_chunks = n_pages // pps
    assert PEER_HEADS % n_chunks == 0 and PAGE_SLOTS - 1 <= n_chunks
    n_steps = max(n_tiles, nb)
    tile = lambda t, pt: jnp.minimum(t, n_tiles - 1)
    row = lambda wd: pl.BlockSpec((None, 1, wd), lambda t, pt: (jnp.minimum(t, nb - 1), 0, 0))
    blk = pl.BlockSpec((HK, LANES), lambda t, pt: (0, tile(t, pt)))
    grid_spec = pltpu.PrefetchScalarGridSpec(
        num_scalar_prefetch=1,
        grid=(n_steps,),
        in_specs=[pl.BlockSpec((2 * PEER_HEADS, PEER_NKEYS, LANES), lambda t, pt: (0, 0, tile(t, pt))),
                  row(DSA_W), row(DSA_W), row(DSA_W), row(am.shape[-1]),
                  pl.BlockSpec((3, DSA_HEADS, LANES), lambda t, pt: (0, 0, 0)),
                  pl.BlockSpec(memory_space=pl.ANY), pl.BlockSpec(memory_space=pl.ANY)],
        out_specs=[blk, blk, row(DSA_W)],
        scratch_shapes=[pltpu.VMEM((PAGE_SLOTS, pps, DSA_HEADS, DSA_DH, page), F32),
                        pltpu.VMEM((PAGE_SLOTS, pps, DSA_HEADS, DSA_DH, page), F32),
                        pltpu.SemaphoreType.DMA((2, PAGE_SLOTS)),
                        pltpu.VMEM((MXU_ROWS, DSA_W), F32), pltpu.VMEM((DSA_HEADS, LANES), F32),
                        pltpu.VMEM((DSA_HEADS, LANES), F32), pltpu.VMEM((DSA_W, MXU_ROWS), F32)],
    )
    r3 = lambda a: a.reshape(nb, 1, a.shape[-1])
    return pl.pallas_call(
        functools.partial(_route_sample_kernel, nb=nb, n_chunks=n_chunks),
        grid_spec=grid_spec,
        out_shape=[jax.ShapeDtypeStruct((HK, n), F32), jax.ShapeDtypeStruct((HK, n), I32),
                   jax.ShapeDtypeStruct((nb, 1, DSA_W), BF16)],
        compiler_params=_params(("arbitrary",)),
        name="route_sample",
    )(page_table, st, r3(dq), r3(dk_new), r3(dv_new), r3(am), bias_rows, ck_t, cv_t)


def _tail1_kernel(ry_ref, do_ref, h_ref, wo_ref, g1_ref, b1_ref, wq_ref, sk_ref, h1_ref, h1b_ref, st_ref):
    half = wo_ref.shape[0] // 2
    mix = (jnp.dot(ry_ref[...], wo_ref[0:half], preferred_element_type=F32)
           + jnp.dot(do_ref[...], wo_ref[half:], preferred_element_type=F32))
    h1 = _ln(DN_ALPHA * h_ref[...] + mix) * g1_ref[...] + b1_ref[...]
    h1_ref[...] = h1
    h1b = h1.astype(BF16)
    h1b_ref[...] = h1b
    q = jnp.dot(h1b, wq_ref[...], preferred_element_type=F32)
    for hh in range(PEER_HEADS):
        qh = _ln(q[:, hh * PEER_DKEY:(hh + 1) * PEER_DKEY]).astype(BF16)
        for s in range(2):
            qs = qh[:, s * (PEER_DKEY // 2):(s + 1) * (PEER_DKEY // 2)]
            st_ref[hh * 2 + s] = lax.dot_general(sk_ref[hh, s], qs, NT, preferred_element_type=F32)


def _tail1(ret_y, dsa_o, h, wo, g1, b1, wq, sk, tm):
    n = h.shape[0]
    row = lambda wd: pl.BlockSpec((tm, wd), lambda i: (i, 0))
    fixed = lambda shp: pl.BlockSpec(shp, lambda i: (0,) * len(shp))
    return pl.pallas_call(
        _tail1_kernel,
        grid=(n // tm,),
        in_specs=[row(512), row(512), row(D_MODEL), fixed(wo.shape), fixed((1, D_MODEL)), fixed((1, D_MODEL)),
                  fixed(wq.shape), fixed(sk.shape)],
        out_specs=[row(D_MODEL), row(D_MODEL),
                   pl.BlockSpec((2 * PEER_HEADS, PEER_NKEYS, tm), lambda i: (0, 0, i))],
        out_shape=[jax.ShapeDtypeStruct((n, D_MODEL), F32), jax.ShapeDtypeStruct((n, D_MODEL), BF16),
                   jax.ShapeDtypeStruct((2 * PEER_HEADS, PEER_NKEYS, n), F32)],
        compiler_params=_params(("parallel",)),
        name="tail1",
    )(ret_y, dsa_o, h, wo, g1, b1, wq, sk)


def _top16(s):
    kio = lax.broadcasted_iota(I32, s.shape, 0).astype(F32)
    vals, idxs = [], []
    for _ in range(PEER_TOPK):
        m = jnp.max(s, axis=0, keepdims=True)
        first = jnp.min(jnp.where(s == m, kio, float(s.shape[0])), axis=0, keepdims=True)
        s = jnp.where(kio == first, -jnp.inf, s)
        vals.append(m)
        idxs.append(first)
    return jnp.concatenate(vals, 0), jnp.concatenate(idxs, 0)


def _route_head(st_ref, g_ref, e_ref, hh):
    r8 = lax.broadcasted_iota(I32, (8, LANES), 0).astype(F32)
    v1, i1 = _top16(st_ref[2 * hh])
    v2, i2 = _top16(st_ref[2 * hh + 1])
    vals = [v1[0:8] + v2[0:1], v1[8:16] + v2[0:1]]
    flat = [r8 * 16.0, (r8 + 8.0) * 16.0]
    exp_id = [i1[0:8] * PEER_NKEYS + i2[0:1], i1[8:16] * PEER_NKEYS + i2[0:1]]
    for j in range(1, 8):
        vals.append(v1[0:8] + v2[j:j + 1])
        flat.append(r8 * 16.0 + float(j))
        exp_id.append(i1[0:8] * PEER_NKEYS + i2[j:j + 1])
    vals.append(v1[0:1] + v2[8:16])
    flat.append(r8 + 8.0)
    exp_id.append(i1[0:1] * PEER_NKEYS + i2[8:16])
    cand = jnp.concatenate(vals, 0)
    pos = jnp.concatenate(flat, 0)
    eid = jnp.concatenate(exp_id, 0)
    tops, es = [], []
    for _ in range(PEER_TOPK):
        m = jnp.max(cand, axis=0, keepdims=True)
        pm = jnp.min(jnp.where(cand == m, pos, 1e9), axis=0, keepdims=True)
        hit = pos == pm
        es.append(jnp.max(jnp.where(hit, eid, -1.0), axis=0, keepdims=True))
        cand = jnp.where(hit, -jnp.inf, cand)
        tops.append(m)
    top = jnp.concatenate(tops, 0)
    ex = jnp.exp(top - top[0:1])
    off = hh * PEER_TOPK if isinstance(hh, int) else pl.multiple_of(hh * PEER_TOPK, PEER_TOPK)
    g_ref[pl.ds(off, PEER_TOPK), :] = ex / jnp.sum(ex, axis=0, keepdims=True)
    e_ref[pl.ds(off, PEER_TOPK), :] = jnp.concatenate(es, 0).astype(I32)


def _route_kernel(st_ref, g_ref, e_ref):
    def head(hh, c):
        _route_head(st_ref, g_ref, e_ref, hh)
        return c

    lax.fori_loop(0, PEER_HEADS, head, 0, unroll=2)


def _route(st):
    n = st.shape[2]
    blk = pl.BlockSpec((HK, LANES), lambda i: (0, i))
    return pl.pallas_call(
        _route_kernel,
        grid=(n // LANES,),
        in_specs=[pl.BlockSpec((2 * PEER_HEADS, PEER_NKEYS, LANES), lambda i: (0, 0, i))],
        out_specs=[blk, blk],
        out_shape=[jax.ShapeDtypeStruct((HK, n), F32), jax.ShapeDtypeStruct((HK, n), I32)],
        compiler_params=_params(("parallel",)),
        name="route",
    )(st)


EXPERT_BLOCK = 1024
EXPERT_SUB = 256


def _peer_act_kernel(x_ref, u_ref, e_ref, g_ref, w_ref, acc_scr):
    j = pl.program_id(1)

    @pl.when(j == 0)
    def _():
        acc_scr[...] = jnp.zeros_like(acc_scr)

    x = x_ref[...]
    e = e_ref[...]
    i2 = e & (PEER_NKEYS - 1)
    i1 = e >> 7
    acc = acc_scr[...]
    for c in range(EXPERT_BLOCK // EXPERT_SUB):
        hmat = lax.dot_general(x, u_ref[c * EXPERT_SUB:(c + 1) * EXPERT_SUB, :].astype(BF16), NT,
                               preferred_element_type=F32)
        for s in range(EXPERT_SUB // PEER_NKEYS):
            got = jnp.take_along_axis(hmat[:, s * PEER_NKEYS:(s + 1) * PEER_NKEYS], i2, axis=1,
                                      mode="promise_in_bounds")
            row1 = (j * EXPERT_BLOCK + c * EXPERT_SUB) // PEER_NKEYS + s
            acc = jnp.where(i1 == row1, got, acc)
    acc_scr[...] = acc

    @pl.when(j == pl.num_programs(1) - 1)
    def _():
        a = acc_scr[...]
        gelu = 0.5 * a * (1.0 + lax.erf(a * (2.0 ** -0.5)))
        w_ref[...] = g_ref[...] * gelu


def _peer_act(xb, u, e, g, tr):
    n = xb.shape[0]
    row = lambda wd: pl.BlockSpec((tr, wd), lambda i, j: (i, 0))
    return pl.pallas_call(
        _peer_act_kernel,
        grid=(n // tr, u.shape[0] // EXPERT_BLOCK),
        in_specs=[row(D_MODEL), pl.BlockSpec((EXPERT_BLOCK, D_MODEL), lambda i, j: (j, 0)), row(HK), row(HK)],
        out_specs=row(HK),
        out_shape=jax.ShapeDtypeStruct((n, HK), F32),
        scratch_shapes=[pltpu.VMEM((tr, HK), F32)],
        compiler_params=_params(("parallel", "arbitrary")),
        name="peer_act",
    )(xb, u, e, g)


GATE_GROUP = 8


def _peer_out_kernel(e_ref, w_ref, v_ref, h1_ref, g2_ref, b2_ref, y_ref, p_scr, acc_scr):
    k = pl.program_id(1)
    tr = e_ref.shape[0]
    rows_per_step = v_ref.shape[0] // PEER_NKEYS
    sub8 = p_scr.shape[1]

    @pl.when(k == 0)
    def _():
        acc_scr[...] = jnp.zeros_like(acc_scr)
        sub = lax.broadcasted_iota(I32, (PEER_NKEYS, HK), 0)

        def body(grp, c):
            r0 = pl.multiple_of(grp * GATE_GROUP, GATE_GROUP)
            ers = e_ref[pl.ds(r0, GATE_GROUP), :]
            wrs = w_ref[pl.ds(r0, GATE_GROUP), :]
            ps = []
            for t in range(GATE_GROUP):
                er = ers[t:t + 1, :]
                wr = wrs[t:t + 1, :]
                o1 = jnp.where(sub == (er >> 7), 1.0, 0.0).astype(BF16)
                o2 = jnp.where(sub == (er & (PEER_NKEYS - 1)), wr, 0.0).astype(BF16)
                ps.append(lax.dot_general(o1, o2, NT, preferred_element_type=F32))
            x = jnp.stack(ps).reshape(GATE_GROUP, PEER_NKEYS // sub8, sub8, PEER_NKEYS)
            p_scr[:, :, pl.ds(r0, GATE_GROUP), :] = pltpu.einshape("tqsl->qstl", x)
            return c

        lax.fori_loop(0, tr // GATE_GROUP, body, 0, unroll=4)

    tot = None
    for j in range(0, rows_per_step, 2):
        q = (k * rows_per_step + j) // sub8
        lhs = jnp.concatenate([p_scr[q, j % sub8], p_scr[q, (j + 1) % sub8]], axis=1).astype(BF16)
        d = jnp.dot(lhs, v_ref[j * PEER_NKEYS:(j + 2) * PEER_NKEYS, :], preferred_element_type=F32)
        tot = d if tot is None else tot + d
    acc_scr[...] += tot

    @pl.when(k == pl.num_programs(1) - 1)
    def _():
        y_ref[...] = _ln(DN_ALPHA * h1_ref[...] + acc_scr[...]) * g2_ref[...] + b2_ref[...]


def _peer_out(e, w, v, h1, g2, b2, tr, kc):
    n = h1.shape[0]
    row = lambda wd: pl.BlockSpec((tr, wd), lambda i, k: (i, 0))
    vec = pl.BlockSpec((1, D_MODEL), lambda i, k: (0, 0))
    return pl.pallas_call(
        _peer_out_kernel,
        grid=(n // tr, v.shape[0] // kc),
        in_specs=[row(HK), row(HK), pl.BlockSpec((kc, D_MODEL), lambda i, k: (k, 0)), row(D_MODEL), vec, vec],
        out_specs=row(D_MODEL),
        out_shape=jax.ShapeDtypeStruct((n, D_MODEL), F32),
        scratch_shapes=[pltpu.VMEM((PEER_NKEYS // GATE_GROUP, GATE_GROUP, tr, PEER_NKEYS), F32),
                        pltpu.VMEM((tr, D_MODEL), F32)],
        compiler_params=_params(("parallel", "arbitrary")),
        name="peer_out",
    )(e, w, v, h1, g2, b2)


def _peer_experts(g_t, e_t, h1, h1b, wts, tr_act, tr_out, kc):
    _, _, _, g2, b2, _, _, u, v = wts
    g = g_t.T
    e = e_t.T
    w = _peer_act(h1b, u, e, g, tr_act)
    return _peer_out(e, w, v, h1, g2, b2, tr_out, kc)


def kernel(x_prompt, x_sample, cache_k, cache_v, cache_kidx, state_ret, page_table, meta_tokens, rel_bias,
           w_in, w_out, ln1_g, ln1_b, ln2_g, ln2_b, peer_wq, peer_subkeys, peer_u, peer_v):
    assert w_in.shape[0] == 1 and x_sample.shape[1] == 1
    nbp, seq, _ = x_prompt.shape
    t = seq + N_META
    nbs = x_sample.shape[0]
    n_pages = page_table.shape[1]
    page = cache_kidx.shape[2]
    past = n_pages * page
    assert page == LANES and n_pages % PAGES_PER_STEP == 0

    hp = jnp.concatenate([jnp.broadcast_to(meta_tokens[None], (nbp, N_META, D_MODEL)), x_prompt], 1)
    hp = hp.reshape(nbp * t, D_MODEL)
    hs = x_sample.reshape(nbs, D_MODEL)

    w = w_in[0]
    wa = w[:, :PROJ_MAIN].astype(BF16)
    wt = jnp.pad(w[:, PROJ_MAIN:], ((0, 0), (0, PROJ_TAIL - (w.shape[1] - PROJ_MAIN)))).astype(BF16)
    wts = (w_out[0].astype(BF16), ln1_g, ln1_b, ln2_g, ln2_b, peer_wq[0].astype(BF16),
           peer_subkeys[0].astype(BF16), peer_u[0], peer_v[0].astype(BF16))
    bias_tiles, bias_rows = _bias_tables(rel_bias)

    rq, rk, rv, rg, dq, dk, dv, iq, tail = _inproj(hp, wa, wt, 384)
    b3 = lambda a: a.reshape(nbp, t, a.shape[-1])
    cosf, sins = _rot_tables(jnp.arange(t, dtype=I32))
    ret_y, ret_s = _ret_prompt(b3(rq), b3(rk), b3(rv), b3(rg), cosf, sins)
    dsa_o = _dsa_prompt(b3(dq), b3(dk), b3(dv), b3(iq), b3(tail), bias_tiles, rel_bias[REL_BUCKETS - 1])
    wo, g1, b1, _, _, wq, sk, _, _ = wts
    h1p, h1bp, stp = _tail1(ret_y.reshape(nbp * t, 512), dsa_o.reshape(nbp * t, DSA_W), hp, wo, g1, b1, wq, sk, 384)

    kidx_t = jnp.transpose(cache_kidx[0], (0, 2, 1))
    ck_t = jnp.transpose(cache_k[0], (0, 2, 3, 1))
    cv_t = jnp.transpose(cache_v[0], (0, 2, 3, 1))
    srq, srk, srv, srg, sdq, sdk, sdv, siq, stail = _inproj(hs, wa, wt, nbs)
    sret_y, sret_s = _ret_sample(srq, srk, srv, srg, state_ret[0], past)
    scores = _idx_scores(page_table, siq.reshape(nbs, DSA_HEADS, IDX_DIM),
                         stail[:, IDX_DIM:IDX_DIM + DSA_HEADS].reshape(nbs, DSA_HEADS, 1),
                         stail[:, :IDX_DIM].reshape(nbs, 1, IDX_DIM), kidx_t)
    n_sel = min(TOPK_MAX, (past + 1) // 4)
    am = _sel_mask(scores.reshape(nbs, past + page), n_sel)
    g_tp, e_tp, sdsa_o = _route_sample(stp, page_table, sdq, sdk, sdv, am, bias_rows, ck_t, cv_t)
    yp = _peer_experts(g_tp, e_tp, h1p, h1bp, wts, 1376, 384, 4096)
    y_prompt = yp.reshape(nbp, t, D_MODEL)[:, N_META:]
    h1s, h1bs, sts = _tail1(sret_y, sdsa_o.reshape(nbs, DSA_W), hs, wo, g1, b1, wq, sk, nbs)
    g_ts, e_ts = _route(sts)
    ys = _peer_experts(g_ts, e_ts, h1s, h1bs, wts, nbs, nbs, 2048)

    kv_p = lambda a: a.reshape(1, nbp, t, DSA_HEADS, DSA_DH)
    kv_s = lambda a: a.reshape(1, nbs, 1, DSA_HEADS, DSA_DH)
    return (y_prompt, ys.reshape(nbs, 1, D_MODEL), kv_p(dk), kv_p(dv),
            tail[:, :IDX_DIM].reshape(1, nbp, t, IDX_DIM), ret_s[None],
            kv_s(sdk), kv_s(sdv), stail[:, :IDX_DIM].reshape(1, nbs, 1, IDX_DIM), sret_s[None])
```
